```python
import math
import jax, jax.numpy as jnp
from jax import lax
import numpy as np

D_MODEL = 2048
BATCH = 4
SEQ = 8192
DEPTH = 1

N_MEM = 256
D_MIX = D_MODEL
RET_WIDTH = D_MIX // 2
LRU_WIDTH = D_MIX - RET_WIDTH
RET_HEADS = 4
RET_HEAD_DIM = RET_WIDTH // RET_HEADS
RET_CHUNK = 128
ROPE_BASE = 10000.0
LRU_BLOCKS = 8
LRU_BLOCK_DIM = LRU_WIDTH // LRU_BLOCKS
LRU_CONV = 4
LRU_C = 8.0
XA_HEADS = 4
XA_HEAD_DIM = D_MODEL // XA_HEADS
D_FF = 5632
FFN_CONV = 3
EPS = 1e-6
IN_COLS = 4 * RET_WIDTH + 2 * LRU_WIDTH

kernel_name = 'hymba_retention_rglru_block'


def rms_norm(x, g):
    xf = x.astype(jnp.float32)
    y = xf * lax.rsqrt(jnp.mean(xf * xf, axis=-1, keepdims=True) + EPS)
    return (y * g.astype(jnp.float32)).astype(x.dtype)


def causal_dwconv(x, w, b):
    K, C = w.shape
    y = lax.conv_general_dilated(x, w[:, None, :].astype(x.dtype), window_strides=(1,),
                                 padding=((K - 1, 0),), dimension_numbers=('NWC', 'WIO', 'NWC'),
                                 feature_group_count=C)
    return y + b.astype(x.dtype)


def rotary(t, pos):
    half = t.shape[-1] // 2
    inv = ROPE_BASE ** (-jnp.arange(half, dtype=jnp.float32) / half)
    ang = pos.astype(jnp.float32)[..., None] * inv
    cos = jnp.cos(ang)[:, :, None, :]
    sin = jnp.sin(ang)[:, :, None, :]
    t1, t2 = t[..., :half], t[..., half:]
    return jnp.concatenate([t1 * cos - t2 * sin, t1 * sin + t2 * cos], axis=-1)


def retention(q, k, v):
    B, S, H, Dk = q.shape
    Dv = v.shape[-1]
    C = RET_CHUNK
    N = S // C
    log_g = jnp.log(1.0 - 2.0 ** (-5.0 - jnp.arange(H, dtype=jnp.float32)))
    idx = jnp.arange(C, dtype=jnp.float32)
    diff = idx[:, None] - idx[None, :]
    intra = jnp.where(diff >= 0, jnp.exp(log_g[:, None, None] * jnp.maximum(diff, 0.0)), 0.0)
    q_dec = jnp.exp(log_g[:, None] * (idx + 1.0))[None, :, :, None]
    k_dec = jnp.exp(log_g[:, None] * (C - 1.0 - idx))[None, :, :, None]
    chunk_dec = jnp.exp(log_g * C)[None, :, None, None]

    def to_chunks(t):
        return t.reshape(B, N, C, H, t.shape[-1]).transpose(1, 0, 3, 2, 4)

    def step(state, qkv):
        qc, kc, vc = qkv
        s = jnp.einsum('bhid,bhjd->bhij', qc, kc) * intra
        inner = jnp.einsum('bhij,bhje->bhie', s, vc)
        cross = jnp.einsum('bhid,bhde->bhie', qc * q_dec, state)
        state = state * chunk_dec + jnp.einsum('bhjd,bhje->bhde', kc * k_dec, vc)
        return state, inner + cross

    state0 = jnp.zeros((B, H, Dk, Dv), jnp.float32)
    _, out = lax.scan(step, state0, (to_chunks(q), to_chunks(k), to_chunks(v)))
    return out.transpose(1, 0, 3, 2, 4).reshape(B, S, H, Dv)


def rg_lru(u, w_a, b_a, w_x, b_x, lam):
    B, S, W = u.shape
    uf = u.astype(jnp.float32)
    ub = uf.reshape(B, S, LRU_BLOCKS, LRU_BLOCK_DIM)
    r = jax.nn.sigmoid(jnp.einsum('bsnc,ncd->bsnd', ub, w_a.astype(jnp.float32)) + b_a.astype(jnp.float32)).reshape(B, S, W)
    i = jax.nn.sigmoid(jnp.einsum('bsnc,ncd->bsnd', ub, w_x.astype(jnp.float32)) + b_x.astype(jnp.float32)).reshape(B, S, W)
    log_a = LRU_C * r * jax.nn.log_sigmoid(lam.astype(jnp.float32))
    a = jnp.exp(log_a)
    b = jnp.sqrt(-jnp.expm1(2.0 * log_a)) * (i * uf)

    def combine(lhs, rhs):
        a1, b1 = lhs
        a2, b2 = rhs
        return a1 * a2, a2 * b1 + b2

    _, h = lax.associative_scan(combine, (a, b), axis=1)
    return h.astype(u.dtype)


def hybrid_mixer(xn, pos, w_in, ret_g, conv_w, conv_b, w_a, b_a, w_x, b_x, lam, w_out):
    B, S, _ = xn.shape
    h = xn @ w_in
    R, L = RET_WIDTH, LRU_WIDTH
    q, k, v, g, u, y = jnp.split(h, [R, 2 * R, 3 * R, 4 * R, 4 * R + L], axis=-1)
    shp = (B, S, RET_HEADS, RET_HEAD_DIM)
    rq = rotary(q.reshape(shp).astype(jnp.float32), pos)
    rk = rotary(k.reshape(shp).astype(jnp.float32), pos) * (RET_HEAD_DIM ** -0.5)
    rv = v.reshape(shp).astype(jnp.float32)
    ret = retention(rq, rk, rv)
    ret = ret * lax.rsqrt(jnp.mean(ret * ret, axis=-1, keepdims=True) + EPS)
    ret = (ret.reshape(B, S, R) * ret_g.astype(jnp.float32) * jax.nn.silu(g.astype(jnp.float32))).astype(xn.dtype)
    uc = causal_dwconv(u, conv_w, conv_b)
    lru = rg_lru(uc, w_a, b_a, w_x, b_x, lam) * jax.nn.gelu(y)
    return jnp.concatenate([ret, lru], axis=-1) @ w_out


def cross_attend(xn, memn, wq, wk, wv, wo):
    B, S, _ = xn.shape
    M = memn.shape[1]
    q = (xn @ wq).reshape(B, S, XA_HEADS, XA_HEAD_DIM)
    k = (memn @ wk).reshape(B, M, XA_HEADS, XA_HEAD_DIM)
    v = (memn @ wv).reshape(B, M, XA_HEADS, XA_HEAD_DIM)
    scores = jnp.einsum('bshd,bmhd->bhsm', q, k).astype(jnp.float32) * (XA_HEAD_DIM ** -0.5)
    p = jax.nn.softmax(scores, axis=-1).astype(v.dtype)
    o = jnp.einsum('bhsm,bmhd->bshd', p, v).reshape(B, S, XA_HEADS * XA_HEAD_DIM)
    return o @ wo


def conv_ffn(xn, w_up, conv_w, conv_b, w_down):
    h = causal_dwconv(xn @ w_up, conv_w, conv_b)
    a, b = jnp.split(h, 2, axis=-1)
    return (jax.nn.silu(a) * b) @ w_down


def setup_inputs(seed: int = 0) -> dict:
    key = jax.random.key(seed)
    ks = jax.random.split(key, 28)
    f32 = jnp.float32

    def nrm(k, shape, fan_in):
        return jax.random.normal(k, shape, f32) * (fan_in ** -0.5)

    def gain(k, shape):
        return 1.0 + 0.02 * jax.random.normal(k, shape, f32)

    def bias(k, shape):
        return 0.01 * jax.random.normal(k, shape, f32)

    x = jax.random.normal(ks[0], (BATCH, SEQ, D_MODEL), f32)
    mem = jax.random.normal(ks[1], (BATCH, N_MEM, D_MODEL), f32)
    offset = jax.random.randint(ks[2], (BATCH, 1), 0, 4096, dtype=jnp.int32)
    positions = (offset + jnp.arange(SEQ, dtype=jnp.int32)[None, :]).astype(jnp.int32)
    u = jax.random.uniform(ks[12], (DEPTH, LRU_WIDTH), f32, minval=0.9, maxval=0.999)
    a0 = u ** (1.0 / LRU_C)
    rg_lambda = jnp.log(a0) - jnp.log1p(-a0)
    return {
        'x': x,
        'mem': mem,
        'positions': positions,
        'norm1_g': gain(ks[3], (DEPTH, D_MODEL)),
        'w_in': nrm(ks[4], (DEPTH, D_MODEL, IN_COLS), D_MODEL),
        'ret_g': gain(ks[5], (DEPTH, RET_WIDTH)),
        'rg_conv_w': nrm(ks[6], (DEPTH, LRU_CONV, LRU_WIDTH), LRU_CONV),
        'rg_conv_b': bias(ks[7], (DEPTH, LRU_WIDTH)),
        'rg_wa': nrm(ks[8], (DEPTH, LRU_BLOCKS, LRU_BLOCK_DIM, LRU_BLOCK_DIM), LRU_BLOCK_DIM),
        'rg_ba': bias(ks[9], (DEPTH, LRU_BLOCKS, LRU_BLOCK_DIM)),
        'rg_wx': nrm(ks[10], (DEPTH, LRU_BLOCKS, LRU_BLOCK_DIM, LRU_BLOCK_DIM), LRU_BLOCK_DIM),
        'rg_bx': bias(ks[11], (DEPTH, LRU_BLOCKS, LRU_BLOCK_DIM)),
        'rg_lambda': rg_lambda,
        'w_out': nrm(ks[13], (DEPTH, D_MIX, D_MODEL), D_MIX),
        'norm2_g': gain(ks[14], (DEPTH, D_MODEL)),
        'norm_mem_g': gain(ks[15], (DEPTH, D_MODEL)),
        'xa_wq': nrm(ks[16], (DEPTH, D_MODEL, D_MODEL), D_MODEL),
        'xa_wk': nrm(ks[17], (DEPTH, D_MODEL, D_MODEL), D_MODEL),
        'xa_wv': nrm(ks[18], (DEPTH, D_MODEL, D_MODEL), D_MODEL),
        'xa_wo': nrm(ks[19], (DEPTH, D_MODEL, D_MODEL), D_MODEL),
        'norm3_g': gain(ks[20], (DEPTH, D_MODEL)),
        'ffn_w_up': nrm(ks[21], (DEPTH, D_MODEL, 2 * D_FF), D_MODEL),
        'ffn_conv_w': nrm(ks[22], (DEPTH, FFN_CONV, 2 * D_FF), FFN_CONV),
        'ffn_conv_b': bias(ks[23], (DEPTH, 2 * D_FF)),
        'ffn_w_down': nrm(ks[24], (DEPTH, D_FF, D_MODEL), D_FF),
        'final_g': gain(ks[25], (D_MODEL,)),
    }


def reference(x, mem, positions, norm1_g, w_in, ret_g, rg_conv_w, rg_conv_b, rg_wa, rg_ba,
              rg_wx, rg_bx, rg_lambda, w_out, norm2_g, norm_mem_g, xa_wq, xa_wk, xa_wv, xa_wo,
              norm3_g, ffn_w_up, ffn_conv_w, ffn_conv_b, ffn_w_down, final_g):
    for l in range(DEPTH):
        x = x + hybrid_mixer(rms_norm(x, norm1_g[l]), positions, w_in[l], ret_g[l],
                             rg_conv_w[l], rg_conv_b[l], rg_wa[l], rg_ba[l], rg_wx[l], rg_bx[l],
                             rg_lambda[l], w_out[l])
        memn = rms_norm(mem, norm_mem_g[l])
        x = x + cross_attend(rms_norm(x, norm2_g[l]), memn, xa_wq[l], xa_wk[l], xa_wv[l], xa_wo[l])
        x = x + conv_ffn(rms_norm(x, norm3_g[l]), ffn_w_up[l], ffn_conv_w[l], ffn_conv_b[l], ffn_w_down[l])
    return rms_norm(x, final_g)
```

```python
import functools
import math

import jax
import jax.numpy as jnp
from jax import lax
from jax.experimental import pallas as pl
from jax.experimental.pallas import tpu as pltpu

F32 = jnp.float32
BF16 = jnp.bfloat16

EPS = 1e-6
ROPE_BASE = 10000.0
RET_HEADS = 4
RET_HEAD_DIM = 256
LRU_BLOCKS = 8
LRU_BLOCK_DIM = 128
LRU_CONV = 4
LRU_C = 8.0
XA_HEADS = 4
FFN_CONV = 3

V7X_LANES = 128
V7X_SUBLANES = 8
V7X_VMEM_LIMIT = 56 * 1024 * 1024

IN_TM = 512
RET_CHUNK = 128
LRU_TT = 512
MID_TM = 256
FFN_TM = 512
FFN_TF = 512
KV_TN = 512


def _params(*sem):
    return pltpu.CompilerParams(dimension_semantics=sem, vmem_limit_bytes=V7X_VMEM_LIMIT)


def _rms(x, g):
    return x * lax.rsqrt(jnp.mean(x * x, axis=-1, keepdims=True) + EPS) * g


def _sigmoid(x):
    return 1.0 / (1.0 + jnp.exp(-x))


def _resident(shape):
    nd = len(shape)
    return pl.BlockSpec(shape, lambda *_: (0,) * nd, pipeline_mode=pl.Buffered(1))


def _in_proj_kernel(x_ref, g_ref, pos_ref, inv_ref, w_ref, o_ref, xn_ref, cos_ref, sin_ref):
    j = pl.program_id(1)

    @pl.when(j == 0)
    def _():
        xn_ref[...] = _rms(x_ref[...], g_ref[...]).astype(BF16)
        ang = pos_ref[...] * inv_ref[...]
        cos_ref[...] = jnp.cos(ang)
        sin_ref[...] = jnp.sin(ang)

    h = jnp.dot(xn_ref[...], w_ref[...], preferred_element_type=F32)

    @pl.when(j < 2)
    def _():
        scale = jnp.where(j == 1, RET_HEAD_DIM ** -0.5, 1.0).astype(F32)
        c = cos_ref[...] * scale
        s = sin_ref[...] * scale
        half = RET_HEAD_DIM // 2
        for hd in range(RET_HEADS):
            lo = hd * RET_HEAD_DIM
            t1 = h[:, lo:lo + half]
            t2 = h[:, lo + half:lo + RET_HEAD_DIM]
            o_ref[:, lo:lo + half] = (t1 * c - t2 * s).astype(BF16)
            o_ref[:, lo + half:lo + RET_HEAD_DIM] = (t1 * s + t2 * c).astype(BF16)

    @pl.when(j >= 2)
    def _():
        o_ref[...] = h.astype(BF16)


def _in_proj(x, g, posf, inv, w, tm):
    T, D = x.shape
    N = w.shape[1]
    tn = RET_HEADS * RET_HEAD_DIM
    return pl.pallas_call(
        _in_proj_kernel,
        out_shape=jax.ShapeDtypeStruct((T, N), BF16),
        grid=(T // tm, N // tn),
        in_specs=[
            pl.BlockSpec((tm, D), lambda i, j: (i, 0)),
            pl.BlockSpec((1, D), lambda i, j: (0, 0)),
            pl.BlockSpec((tm, V7X_LANES), lambda i, j: (i, 0)),
            pl.BlockSpec((1, V7X_LANES), lambda i, j: (0, 0)),
            pl.BlockSpec((D, tn), lambda i, j: (0, j)),
        ],
        out_specs=pl.BlockSpec((tm, tn), lambda i, j: (i, j)),
        scratch_shapes=[
            pltpu.VMEM((tm, D), BF16),
            pltpu.VMEM((tm, V7X_LANES), F32),
            pltpu.VMEM((tm, V7X_LANES), F32),
        ],
        compiler_params=_params("arbitrary", "arbitrary"),
        name="in_proj",
    )(x, g, posf, inv, w)


def _retention_kernel(q_ref, k_ref, v_ref, g_ref, intra_ref, qdec_ref, kdec_ref, cdec_ref, rg_ref,
                      o_ref, state_ref):
    n = pl.program_id(1)

    @pl.when(n == 0)
    def _():
        state_ref[...] = jnp.zeros_like(state_ref)

    dh = RET_HEAD_DIM
    for hd in range(RET_HEADS):
        sl = slice(hd * dh, (hd + 1) * dh)
        q = q_ref[:, sl]
        k = k_ref[:, sl]
        v = v_ref[:, sl]
        state = state_ref[hd]
        s = lax.dot_general(q, k, (((1,), (1,)), ((), ())), preferred_element_type=F32) * intra_ref[hd]
        inner = jnp.dot(s.astype(BF16), v, preferred_element_type=F32)
        cross = jnp.dot((q.astype(F32) * qdec_ref[hd]).astype(BF16), state.astype(BF16),
                        preferred_element_type=F32)
        kd = (k.astype(F32) * kdec_ref[hd]).astype(BF16)
        state_ref[hd] = state * cdec_ref[hd] + lax.dot_general(
            kd, v, (((0,), (0,)), ((), ())), preferred_element_type=F32)
        ret = inner + cross
        ret = ret * lax.rsqrt(jnp.mean(ret * ret, axis=-1, keepdims=True) + EPS)
        gate = g_ref[:, sl].astype(F32)
        o_ref[:, sl] = (ret * rg_ref[:, sl] * (gate * _sigmoid(gate))).astype(BF16)


def _retention(h, ret_g, B, S, C):
    T = h.shape[0]
    R = RET_HEADS * RET_HEAD_DIM
    N = S // C
    H, dh = RET_HEADS, RET_HEAD_DIM
    log_g = jnp.log(1.0 - 2.0 ** (-5.0 - jnp.arange(H, dtype=F32)))
    idx = jnp.arange(C, dtype=F32)
    diff = idx[:, None] - idx[None, :]
    intra = jnp.where(diff >= 0, jnp.exp(log_g[:, None, None] * jnp.maximum(diff, 0.0)), 0.0)
    q_dec = jnp.broadcast_to(jnp.exp(log_g[:, None] * (idx + 1.0))[:, :, None], (H, C, dh))
    k_dec = jnp.broadcast_to(jnp.exp(log_g[:, None] * (C - 1.0 - idx))[:, :, None], (H, C, dh))
    c_dec = jnp.broadcast_to(jnp.exp(log_g * C)[:, None, None], (H, dh, dh))

    def col(c):
        return pl.BlockSpec((C, R), lambda b, n: (b * N + n, c))

    return pl.pallas_call(
        _retention_kernel,
        out_shape=jax.ShapeDtypeStruct((T, R), BF16),
        grid=(B, N),
        in_specs=[
            col(0), col(1), col(2), col(3),
            pl.BlockSpec((H, C, C), lambda b, n: (0, 0, 0)),
            pl.BlockSpec((H, C, dh), lambda b, n: (0, 0, 0)),
            pl.BlockSpec((H, C, dh), lambda b, n: (0, 0, 0)),
            pl.BlockSpec((H, dh, dh), lambda b, n: (0, 0, 0)),
            pl.BlockSpec((1, R), lambda b, n: (0, 0)),
        ],
        out_specs=pl.BlockSpec((C, R), lambda b, n: (b * N + n, 0)),
        scratch_shapes=[pltpu.VMEM((H, dh, dh), F32)],
        compiler_params=_params("arbitrary", "arbitrary"),
        name="retention",
    )(h, h, h, h, intra, q_dec, k_dec, c_dec, ret_g)


def _rglru_kernel(u_ref, y_ref, cw_ref, cb_ref, wg_ref, ba_ref, bx_ref, lam_ref,
                  o_ref, ubuf_ref, a_ref, b_ref, hcar_ref):
    n = pl.program_id(1)
    tt = u_ref.shape[0]
    W = u_ref.shape[1]
    halo = V7X_SUBLANES

    @pl.when(n == 0)
    def _():
        ubuf_ref[0:halo, :] = jnp.zeros((halo, W), F32)
        hcar_ref[...] = jnp.zeros_like(hcar_ref)

    @pl.when(n > 0)
    def _():
        ubuf_ref[0:halo, :] = ubuf_ref[tt:tt + halo, :]

    ubuf_ref[halo:halo + tt, :] = u_ref[...].astype(F32)

    uc = cb_ref[...] + cw_ref[LRU_CONV - 1:LRU_CONV, :] * ubuf_ref[halo:halo + tt, :]
    for kk in range(LRU_CONV - 1):
        off = halo - (LRU_CONV - 1) + kk
        uc = uc + cw_ref[kk:kk + 1, :] * ubuf_ref[off:off + tt, :]

    lam = lam_ref[...]
    log_sig = jnp.minimum(lam, 0.0) - jnp.log1p(jnp.exp(-jnp.abs(lam)))
    bd = LRU_BLOCK_DIM
    for nb in range(LRU_BLOCKS):
        sl = slice(nb * bd, (nb + 1) * bd)
        ub = uc[:, sl]
        z = jnp.dot(ub.astype(BF16), wg_ref[nb], preferred_element_type=F32)
        r = _sigmoid(z[:, :bd] + ba_ref[:, sl])
        ig = _sigmoid(z[:, bd:] + bx_ref[:, sl])
        log_a = LRU_C * r * log_sig[:, sl]
        a_ref[:, sl] = jnp.exp(log_a)
        th = jnp.tanh(log_a)
        b_ref[:, sl] = jnp.sqrt(-2.0 * th / (1.0 - th)) * (ig * ub)

    row = lax.broadcasted_iota(jnp.int32, (V7X_SUBLANES, W), 0)

    def group(gi, hprev):
        r0 = pl.multiple_of(gi * V7X_SUBLANES, V7X_SUBLANES)
        a = a_ref[pl.ds(r0, V7X_SUBLANES), :]
        b = b_ref[pl.ds(r0, V7X_SUBLANES), :]
        for sh in (1, 2, 4):
            keep = row >= sh
            a_sh = jnp.where(keep, pltpu.roll(a, sh, 0), 1.0)
            b_sh = jnp.where(keep, pltpu.roll(b, sh, 0), 0.0)
            b = a * b_sh + b
            a = a * a_sh
        hcur = b + a * hprev
        b_ref[pl.ds(r0, V7X_SUBLANES), :] = hcur
        return jnp.broadcast_to(hcur[V7X_SUBLANES - 1:V7X_SUBLANES, :], (V7X_SUBLANES, W))

    hlast = lax.fori_loop(0, tt // V7X_SUBLANES, group, hcar_ref[...], unroll=4)
    hcar_ref[...] = hlast

    y = y_ref[...].astype(F32)
    gelu = 0.5 * y * (1.0 + jnp.tanh(math.sqrt(2.0 / math.pi) * (y + 0.044715 * (y * y * y))))
    o_ref[...] = (b_ref[...] * gelu).astype(BF16)


def _rglru(h, conv_w, conv_b, wg, ba, bx, lam, B, S, tt):
    T = h.shape[0]
    W = LRU_BLOCKS * LRU_BLOCK_DIM
    N = S // tt

    def const(shape):
        nd = len(shape)
        return pl.BlockSpec(shape, lambda b, n: (0,) * nd)

    return pl.pallas_call(
        _rglru_kernel,
        out_shape=jax.ShapeDtypeStruct((T, W), BF16),
        grid=(B, N),
        in_specs=[
            pl.BlockSpec((tt, W), lambda b, n: (b * N + n, 4)),
            pl.BlockSpec((tt, W), lambda b, n: (b * N + n, 5)),
            const((LRU_CONV, W)), const((1, W)),
            const((LRU_BLOCKS, LRU_BLOCK_DIM, 2 * LRU_BLOCK_DIM)),
            const((1, W)), const((1, W)), const((1, W)),
        ],
        out_specs=pl.BlockSpec((tt, W), lambda b, n: (b * N + n, 0)),
        scratch_shapes=[
            pltpu.VMEM((tt + V7X_SUBLANES, W), F32),
            pltpu.VMEM((tt, W), F32),
            pltpu.VMEM((tt, W), F32),
            pltpu.VMEM((V7X_SUBLANES, W), F32),
        ],
        compiler_params=_params("arbitrary", "arbitrary"),
        name="rglru",
    )(h, h, conv_w, conv_b, wg, ba, bx, lam)


def _kv_proj_kernel(m_ref, g_ref, wk_ref, wv_ref, k_ref, v_ref, mn_ref):
    @pl.when(pl.program_id(0) == 0)
    def _():
        mn_ref[...] = _rms(m_ref[...], g_ref[...]).astype(BF16)

    mn = mn_ref[...]
    k_ref[...] = jnp.dot(mn, wk_ref[...], preferred_element_type=F32).astype(BF16)
    v_ref[...] = jnp.dot(mn, wv_ref[...], preferred_element_type=F32).astype(BF16)


def _kv_proj(mem, g, wk, wv, tn):
    M, D = mem.shape
    N = wk.shape[1]
    return pl.pallas_call(
        _kv_proj_kernel,
        out_shape=(jax.ShapeDtypeStruct((M, N), BF16), jax.ShapeDtypeStruct((M, N), BF16)),
        grid=(N // tn,),
        in_specs=[
            pl.BlockSpec((M, D), lambda j: (0, 0)),
            pl.BlockSpec((1, D), lambda j: (0, 0)),
            pl.BlockSpec((D, tn), lambda j: (0, j)),
            pl.BlockSpec((D, tn), lambda j: (0, j)),
        ],
        out_specs=(pl.BlockSpec((M, tn), lambda j: (0, j)), pl.BlockSpec((M, tn), lambda j: (0, j))),
        scratch_shapes=[pltpu.VMEM((M, D), BF16)],
        compiler_params=_params("arbitrary"),
        name="kv_proj",
    )(mem, g, wk, wv)


def _mid_kernel(x_ref, ret_ref, lru_ref, wout_ref, g2_ref, wq_ref, k_ref, v_ref, wo_ref, o_ref, att_ref):
    R = ret_ref.shape[1]
    x1 = (x_ref[...]
          + jnp.dot(ret_ref[...], wout_ref[0:R, :], preferred_element_type=F32)
          + jnp.dot(lru_ref[...], wout_ref[R:, :], preferred_element_type=F32))
    xn = _rms(x1, g2_ref[...]).astype(BF16)
    q = jnp.dot(xn, wq_ref[...], preferred_element_type=F32).astype(BF16)
    D = q.shape[1]
    dh = D // XA_HEADS
    scale = dh ** -0.5
    for hd in range(XA_HEADS):
        sl = slice(hd * dh, (hd + 1) * dh)
        s = lax.dot_general(q[:, sl], k_ref[:, sl], (((1,), (1,)), ((), ())),
                            preferred_element_type=F32) * scale
        e = jnp.exp(s - jnp.max(s, axis=-1, keepdims=True))
        p = e / jnp.sum(e, axis=-1, keepdims=True)
        att_ref[:, sl] = jnp.dot(p.astype(BF16), v_ref[:, sl], preferred_element_type=F32).astype(BF16)
    o_ref[...] = x1 + jnp.dot(att_ref[...], wo_ref[...], preferred_element_type=F32)


def _mid(x, ret, lru, w_out, g2, wq, kmem, vmem, wo, B, S, tm):
    T, D = x.shape
    R = ret.shape[1]
    M = kmem.shape[0] // B
    per_b = S // tm
    return pl.pallas_call(
        _mid_kernel,
        out_shape=jax.ShapeDtypeStruct((T, D), F32),
        grid=(T // tm,),
        in_specs=[
            pl.BlockSpec((tm, D), lambda i: (i, 0)),
            pl.BlockSpec((tm, R), lambda i: (i, 0)),
            pl.BlockSpec((tm, R), lambda i: (i, 0)),
            _resident((D, D)),
            pl.BlockSpec((1, D), lambda i: (0, 0)),
            _resident((D, D)),
            pl.BlockSpec((M, D), lambda i: (i // per_b, 0)),
            pl.BlockSpec((M, D), lambda i: (i // per_b, 0)),
            _resident((D, D)),
        ],
        out_specs=pl.BlockSpec((tm, D), lambda i: (i, 0)),
        scratch_shapes=[pltpu.VMEM((tm, D), BF16)],
        compiler_params=_params("arbitrary"),
        name="mid",
    )(x, ret, lru, w_out, g2, wq, kmem, vmem, wo)


def _ffn_kernel(x_ref, g3_ref, wa_ref, wb_ref, cwa_ref, cwb_ref, cba_ref, cbb_ref, wd_ref, gf_ref,
                o_ref, xn_ref, acc_ref, ha_ref, hb_ref, taila_ref, tailb_ref, *, tiles_per_seq):
    i = pl.program_id(0)
    j = pl.program_id(1)
    nj = pl.num_programs(1)
    tm = x_ref.shape[0]
    halo = V7X_SUBLANES

    @pl.when(j == 0)
    def _():
        xn_ref[...] = _rms(x_ref[...], g3_ref[...]).astype(BF16)

    xn = xn_ref[...]
    first = (i % tiles_per_seq) == 0

    def conv(w_ref, cw_ref, cb_ref, hbuf_ref, tail_ref):
        hbuf_ref[halo:halo + tm, :] = jnp.dot(xn, w_ref[...], preferred_element_type=F32)

        @pl.when(first)
        def _():
            hbuf_ref[0:halo, :] = jnp.zeros((halo, hbuf_ref.shape[1]), F32)

        @pl.when(jnp.logical_not(first))
        def _():
            hbuf_ref[0:halo, :] = tail_ref[j]

        tail_ref[j] = hbuf_ref[tm:tm + halo, :]
        y = cb_ref[...] + cw_ref[FFN_CONV - 1:FFN_CONV, :] * hbuf_ref[halo:halo + tm, :]
        for kk in range(FFN_CONV - 1):
            off = halo - (FFN_CONV - 1) + kk
            y = y + cw_ref[kk:kk + 1, :] * hbuf_ref[off:off + tm, :]
        return y

    ya = conv(wa_ref, cwa_ref, cba_ref, ha_ref, taila_ref)
    yb = conv(wb_ref, cwb_ref, cbb_ref, hb_ref, tailb_ref)
    gated = (ya * _sigmoid(ya) * yb).astype(BF16)
    part = jnp.dot(gated, wd_ref[...], preferred_element_type=F32)

    @pl.when(j == 0)
    def _():
        acc_ref[...] = part

    @pl.when(j > 0)
    def _():
        acc_ref[...] += part

    @pl.when(j == nj - 1)
    def _():
        o_ref[...] = _rms(x_ref[...] + acc_ref[...], gf_ref[...])


def _ffn(x, g3, w_up, conv_w, conv_b, w_down, gf, S, tm, tf):
    T, D = x.shape
    F = w_down.shape[0]
    nj = F // tf
    kernel = functools.partial(_ffn_kernel, tiles_per_seq=S // tm)
    return pl.pallas_call(
        kernel,
        out_shape=jax.ShapeDtypeStruct((T, D), F32),
        grid=(T // tm, nj),
        in_specs=[
            pl.BlockSpec((tm, D), lambda i, j: (i, 0)),
            pl.BlockSpec((1, D), lambda i, j: (0, 0)),
            pl.BlockSpec((D, tf), lambda i, j: (0, j)),
            pl.BlockSpec((D, tf), lambda i, j: (0, j + nj)),
            pl.BlockSpec((FFN_CONV, tf), lambda i, j: (0, j)),
            pl.BlockSpec((FFN_CONV, tf), lambda i, j: (0, j + nj)),
            pl.BlockSpec((1, tf), lambda i, j: (0, j)),
            pl.BlockSpec((1, tf), lambda i, j: (0, j + nj)),
            pl.BlockSpec((tf, D), lambda i, j: (j, 0)),
            pl.BlockSpec((1, D), lambda i, j: (0, 0)),
        ],
        out_specs=pl.BlockSpec((tm, D), lambda i, j: (i, 0)),
        scratch_shapes=[
            pltpu.VMEM((tm, D), BF16),
            pltpu.VMEM((tm, D), F32),
            pltpu.VMEM((tm + V7X_SUBLANES, tf), F32),
            pltpu.VMEM((tm + V7X_SUBLANES, tf), F32),
            pltpu.VMEM((nj, V7X_SUBLANES, tf), F32),
            pltpu.VMEM((nj, V7X_SUBLANES, tf), F32),
        ],
        compiler_params=_params("arbitrary", "arbitrary"),
        name="ffn",
    )(x, g3, w_up, w_up, conv_w, conv_w, conv_b, conv_b, w_down, gf)


def _layer(x, mem, posf, inv, B, S, norm1_g, w_in, ret_g, rg_conv_w, rg_conv_b, rg_wa, rg_ba, rg_wx, rg_bx,
           rg_lambda, w_out, norm2_g, norm_mem_g, xa_wq, xa_wk, xa_wv, xa_wo, norm3_g, ffn_w_up,
           ffn_conv_w, ffn_conv_b, ffn_w_down, out_g, tiles):
    W = LRU_BLOCKS * LRU_BLOCK_DIM
    row = lambda a: a.reshape(1, -1)
    h = _in_proj(x, row(norm1_g), posf, inv, w_in.astype(BF16), tiles["in_tm"])
    ret = _retention(h, row(ret_g), B, S, tiles["ret_chunk"])
    wg = jnp.concatenate([rg_wa, rg_wx], axis=-1).astype(BF16)
    lru = _rglru(h, rg_conv_w, row(rg_conv_b), wg, rg_ba.reshape(1, W), rg_bx.reshape(1, W),
                 row(rg_lambda), B, S, tiles["lru_tt"])
    kmem, vmem = _kv_proj(mem, row(norm_mem_g), xa_wk.astype(BF16), xa_wv.astype(BF16), tiles["kv_tn"])
    x2 = _mid(x, ret, lru, w_out.astype(BF16), row(norm2_g), xa_wq.astype(BF16), kmem, vmem,
              xa_wo.astype(BF16), B, S, tiles["mid_tm"])
    return _ffn(x2, row(norm3_g), ffn_w_up.astype(BF16), ffn_conv_w, row(ffn_conv_b),
                ffn_w_down.astype(BF16), out_g, S, tiles["ffn_tm"], tiles["ffn_tf"])


_TILES = dict(in_tm=IN_TM, ret_chunk=RET_CHUNK, lru_tt=LRU_TT, kv_tn=KV_TN, mid_tm=MID_TM,
              ffn_tm=FFN_TM, ffn_tf=FFN_TF)


def _block(x, mem, positions, norm1_g, w_in, ret_g, rg_conv_w, rg_conv_b, rg_wa, rg_ba, rg_wx, rg_bx,
           rg_lambda, w_out, norm2_g, norm_mem_g, xa_wq, xa_wk, xa_wv, xa_wo, norm3_g, ffn_w_up,
           ffn_conv_w, ffn_conv_b, ffn_w_down, final_g, tiles):
    B, S, D = x.shape
    assert w_in.shape[0] == 1, "only depth 1 is supported"
    T = B * S
    half = RET_HEAD_DIM // 2
    inv = (ROPE_BASE ** (-jnp.arange(half, dtype=F32) / half)).reshape(1, half)
    posf = jnp.broadcast_to(positions.reshape(T, 1).astype(F32), (T, half))
    out = _layer(x.reshape(T, D), mem.reshape(B * mem.shape[1], D), posf, inv, B, S, norm1_g[0], w_in[0],
                 ret_g[0], rg_conv_w[0], rg_conv_b[0], rg_wa[0], rg_ba[0], rg_wx[0], rg_bx[0], rg_lambda[0],
                 w_out[0], norm2_g[0], norm_mem_g[0], xa_wq[0], xa_wk[0], xa_wv[0], xa_wo[0], norm3_g[0],
                 ffn_w_up[0], ffn_conv_w[0], ffn_conv_b[0], ffn_w_down[0], final_g.reshape(1, D), tiles)
    return out.reshape(B, S, D)


def kernel(x, mem, positions, norm1_g, w_in, ret_g, rg_conv_w, rg_conv_b, rg_wa, rg_ba, rg_wx, rg_bx, rg_lambda, w_out, norm2_g, norm_mem_g, xa_wq, xa_wk, xa_wv, xa_wo, norm3_g, ffn_w_up, ffn_conv_w, ffn_conv_b, ffn_w_down, final_g):
    return _block(x, mem, positions, norm1_g, w_in, ret_g, rg_conv_w, rg_conv_b, rg_wa, rg_ba, rg_wx, rg_bx,
                  rg_lambda, w_out, norm2_g, norm_mem_g, xa_wq, xa_wk, xa_wv, xa_wo, norm3_g, ffn_w_up,
                  ffn_conv_w, ffn_conv_b, ffn_w_down, final_g, _TILES)
```

```python
import functools
import math

import jax
import jax.numpy as jnp
from jax import lax
from jax.experimental import pallas as pl
from jax.experimental.pallas import tpu as pltpu

F32 = jnp.float32
BF16 = jnp.bfloat16

EPS = 1e-6
ROPE_BASE = 10000.0
RET_HEADS = 4
RET_HEAD_DIM = 256
LRU_BLOCKS = 8
LRU_BLOCK_DIM = 128
LRU_CONV = 4
LRU_C = 8.0
XA_HEADS = 4
FFN_CONV = 3

V7X_LANES = 128
V7X_SUBLANES = 8
V7X_VMEM_LIMIT = 56 * 1024 * 1024

IN_TM = 512
RET_CHUNK = 256
LRU_TT = 512
MID_TM = 256
FFN_TM = 512
FFN_TF = 512
FFN_SUB = 256
KV_TN = 512


def _params(*sem):
    return pltpu.CompilerParams(dimension_semantics=sem, vmem_limit_bytes=V7X_VMEM_LIMIT)


def _rms(x, g):
    return x * lax.rsqrt(jnp.mean(x * x, axis=-1, keepdims=True) + EPS) * g


def _sigmoid(x):
    return 1.0 / (1.0 + jnp.exp(-x))


def _resident(shape):
    nd = len(shape)
    return pl.BlockSpec(shape, lambda *_: (0,) * nd, pipeline_mode=pl.Buffered(1))


def _in_proj_kernel(x_ref, g_ref, pos_ref, inv_ref, dec_ref, w_ref, o_ref, xn_ref, cos_ref, sin_ref):
    j = pl.program_id(1)

    @pl.when(j == 0)
    def _():
        xn_ref[...] = _rms(x_ref[...], g_ref[...]).astype(BF16)
        ang = pos_ref[...] * inv_ref[...]
        cos_ref[...] = jnp.cos(ang)
        sin_ref[...] = jnp.sin(ang)

    h = jnp.dot(xn_ref[...], w_ref[...], preferred_element_type=F32)

    is_qk = j < 2
    half = RET_HEAD_DIM // 2
    for hd in range(RET_HEADS):
        d = dec_ref[0, :, hd * half:(hd + 1) * half]
        c = jnp.where(is_qk, cos_ref[...] * d, 1.0)
        s = jnp.where(is_qk, sin_ref[...] * d, 0.0)
        lo = hd * RET_HEAD_DIM
        t1 = h[:, lo:lo + half]
        t2 = h[:, lo + half:lo + RET_HEAD_DIM]
        o_ref[:, lo:lo + half] = (t1 * c - t2 * s).astype(BF16)
        o_ref[:, lo + half:lo + RET_HEAD_DIM] = (t1 * s + t2 * c).astype(BF16)


def _retention_log_gamma():
    return jnp.log(1.0 - 2.0 ** (-5.0 - jnp.arange(RET_HEADS, dtype=F32)))


def _in_proj(x, g, posf, inv, w, tm, C):
    T, D = x.shape
    N = w.shape[1]
    tn = RET_HEADS * RET_HEAD_DIM
    half = RET_HEAD_DIM // 2
    assert tm % C == 0
    t = (jnp.arange(tm) % C).astype(F32)
    log_g = _retention_log_gamma()
    q_dec = jnp.exp(log_g[None, :] * (t[:, None] + 1.0))
    k_dec = jnp.exp(log_g[None, :] * (C - 1.0 - t[:, None])) * (RET_HEAD_DIM ** -0.5)
    dec = jnp.repeat(jnp.stack([q_dec, k_dec]), half, axis=-1)
    return pl.pallas_call(
        _in_proj_kernel,
        out_shape=jax.ShapeDtypeStruct((T, N), BF16),
        grid=(T // tm, N // tn),
        in_specs=[
            pl.BlockSpec((tm, D), lambda i, j: (i, 0)),
            pl.BlockSpec((1, D), lambda i, j: (0, 0)),
            pl.BlockSpec((tm, half), lambda i, j: (i, 0)),
            pl.BlockSpec((1, half), lambda i, j: (0, 0)),
            pl.BlockSpec((1, tm, RET_HEADS * half), lambda i, j: (jnp.minimum(j, 1), 0, 0)),
            pl.BlockSpec((D, tn), lambda i, j: (0, j)),
        ],
        out_specs=pl.BlockSpec((tm, tn), lambda i, j: (i, j)),
        scratch_shapes=[
            pltpu.VMEM((tm, D), BF16),
            pltpu.VMEM((tm, half), F32),
            pltpu.VMEM((tm, half), F32),
        ],
        compiler_params=_params("arbitrary", "arbitrary"),
        name="in_proj",
    )(x, g, posf, inv, dec, w)


def _retention_kernel(q_ref, k_ref, v_ref, g_ref, mask_ref, cdec_ref, rg_ref, o_ref, state_ref):
    n = pl.program_id(1)

    @pl.when(n == 0)
    def _():
        state_ref[...] = jnp.zeros_like(state_ref)

    dh = RET_HEAD_DIM
    for hd in range(RET_HEADS):
        sl = slice(hd * dh, (hd + 1) * dh)
        q = q_ref[:, sl]
        k = k_ref[:, sl]
        v = v_ref[:, sl]
        state = state_ref[hd]
        s = lax.dot_general(q, k, (((1,), (1,)), ((), ())), preferred_element_type=F32) * mask_ref[hd]
        ret = (jnp.dot(s.astype(BF16), v, preferred_element_type=F32)
               + jnp.dot(q, state.astype(BF16), preferred_element_type=F32))
        state_ref[hd] = state * cdec_ref[hd] + lax.dot_general(
            k, v, (((0,), (0,)), ((), ())), preferred_element_type=F32)
        ret = ret * lax.rsqrt(jnp.mean(ret * ret, axis=-1, keepdims=True) + EPS)
        gate = g_ref[:, sl].astype(F32)
        o_ref[:, sl] = (ret * rg_ref[:, sl] * (gate * _sigmoid(gate))).astype(BF16)


def _retention(h, ret_g, B, S, C):
    T = h.shape[0]
    R = RET_HEADS * RET_HEAD_DIM
    N = S // C
    H, dh = RET_HEADS, RET_HEAD_DIM
    log_g = _retention_log_gamma()
    idx = jnp.arange(C)
    mask = jnp.where(idx[:, None] >= idx[None, :], jnp.exp(-log_g * C)[:, None, None], 0.0)
    c_dec = jnp.broadcast_to(jnp.exp(log_g * C)[:, None, None], (H, 1, dh))

    def col(c):
        return pl.BlockSpec((C, R), lambda b, n: (b * N + n, c))

    return pl.pallas_call(
        _retention_kernel,
        out_shape=jax.ShapeDtypeStruct((T, R), BF16),
        grid=(B, N),
        in_specs=[
            col(0), col(1), col(2), col(3),
            pl.BlockSpec((H, C, C), lambda b, n: (0, 0, 0)),
            pl.BlockSpec((H, 1, dh), lambda b, n: (0, 0, 0)),
            pl.BlockSpec((1, R), lambda b, n: (0, 0)),
        ],
        out_specs=pl.BlockSpec((C, R), lambda b, n: (b * N + n, 0)),
        scratch_shapes=[pltpu.VMEM((H, dh, dh), F32)],
        compiler_params=_params("arbitrary", "arbitrary"),
        name="retention",
    )(h, h, h, h, mask, c_dec, ret_g)


def _rglru_kernel(u_ref, y_ref, cw_ref, cb_ref, wg_ref, ba_ref, bx_ref, lam_ref,
                  o_ref, ubuf_ref, a_ref, b_ref, hcar_ref):
    n = pl.program_id(1)
    tt = u_ref.shape[0]
    W = u_ref.shape[1]
    halo = V7X_SUBLANES

    @pl.when(n == 0)
    def _():
        ubuf_ref[0:halo, :] = jnp.zeros((halo, W), F32)
        hcar_ref[...] = jnp.zeros_like(hcar_ref)

    @pl.when(n > 0)
    def _():
        ubuf_ref[0:halo, :] = ubuf_ref[tt:tt + halo, :]

    ubuf_ref[halo:halo + tt, :] = u_ref[...].astype(F32)

    uc = cb_ref[...] + cw_ref[LRU_CONV - 1:LRU_CONV, :] * ubuf_ref[halo:halo + tt, :]
    for kk in range(LRU_CONV - 1):
        off = halo - (LRU_CONV - 1) + kk
        uc = uc + cw_ref[kk:kk + 1, :] * ubuf_ref[off:off + tt, :]

    lam = lam_ref[...]
    log_sig = jnp.minimum(lam, 0.0) - jnp.log1p(jnp.exp(-jnp.abs(lam)))
    bd = LRU_BLOCK_DIM
    for nb in range(LRU_BLOCKS):
        sl = slice(nb * bd, (nb + 1) * bd)
        ub = uc[:, sl]
        z = jnp.dot(ub.astype(BF16), wg_ref[nb], preferred_element_type=F32)
        r = _sigmoid(z[:, :bd] + ba_ref[:, sl])
        ig = _sigmoid(z[:, bd:] + bx_ref[:, sl])
        log_a = LRU_C * r * log_sig[:, sl]
        a_ref[:, sl] = jnp.exp(log_a)
        th = jnp.tanh(log_a)
        b_ref[:, sl] = jnp.sqrt(-2.0 * th / (1.0 - th)) * (ig * ub)

    row = lax.broadcasted_iota(jnp.int32, (V7X_SUBLANES, W), 0)

    def group(gi, hprev):
        r0 = pl.multiple_of(gi * V7X_SUBLANES, V7X_SUBLANES)
        a = a_ref[pl.ds(r0, V7X_SUBLANES), :]
        b = b_ref[pl.ds(r0, V7X_SUBLANES), :]
        for sh in (1, 2, 4):
            keep = row >= sh
            a_sh = jnp.where(keep, pltpu.roll(a, sh, 0), 1.0)
            b_sh = jnp.where(keep, pltpu.roll(b, sh, 0), 0.0)
            b = a * b_sh + b
            a = a * a_sh
        hcur = b + a * hprev
        b_ref[pl.ds(r0, V7X_SUBLANES), :] = hcur
        return jnp.broadcast_to(hcur[V7X_SUBLANES - 1:V7X_SUBLANES, :], (V7X_SUBLANES, W))

    hlast = lax.fori_loop(0, tt // V7X_SUBLANES, group, hcar_ref[...], unroll=4)
    hcar_ref[...] = hlast

    y = y_ref[...].astype(F32)
    gelu = 0.5 * y * (1.0 + jnp.tanh(math.sqrt(2.0 / math.pi) * (y + 0.044715 * (y * y * y))))
    o_ref[...] = (b_ref[...] * gelu).astype(BF16)


def _rglru(h, conv_w, conv_b, wg, ba, bx, lam, B, S, tt):
    T = h.shape[0]
    W = LRU_BLOCKS * LRU_BLOCK_DIM
    N = S // tt

    def const(shape):
        nd = len(shape)
        return pl.BlockSpec(shape, lambda b, n: (0,) * nd)

    return pl.pallas_call(
        _rglru_kernel,
        out_shape=jax.ShapeDtypeStruct((T, W), BF16),
        grid=(B, N),
        in_specs=[
            pl.BlockSpec((tt, W), lambda b, n: (b * N + n, 4)),
            pl.BlockSpec((tt, W), lambda b, n: (b * N + n, 5)),
            const((LRU_CONV, W)), const((1, W)),
            const((LRU_BLOCKS, LRU_BLOCK_DIM, 2 * LRU_BLOCK_DIM)),
            const((1, W)), const((1, W)), const((1, W)),
        ],
        out_specs=pl.BlockSpec((tt, W), lambda b, n: (b * N + n, 0)),
        scratch_shapes=[
            pltpu.VMEM((tt + V7X_SUBLANES, W), F32),
            pltpu.VMEM((tt, W), F32),
            pltpu.VMEM((tt, W), F32),
            pltpu.VMEM((V7X_SUBLANES, W), F32),
        ],
        compiler_params=_params("arbitrary", "arbitrary"),
        name="rglru",
    )(h, h, conv_w, conv_b, wg, ba, bx, lam)


def _kv_proj_kernel(m_ref, g_ref, wk_ref, wv_ref, k_ref, v_ref, mn_ref):
    @pl.when(pl.program_id(0) == 0)
    def _():
        mn_ref[...] = _rms(m_ref[...], g_ref[...]).astype(BF16)

    mn = mn_ref[...]
    k_ref[...] = jnp.dot(mn, wk_ref[...], preferred_element_type=F32).astype(BF16)
    v_ref[...] = jnp.dot(mn, wv_ref[...], preferred_element_type=F32).astype(BF16)


def _kv_proj(mem, g, wk, wv, tn):
    M, D = mem.shape
    N = wk.shape[1]
    return pl.pallas_call(
        _kv_proj_kernel,
        out_shape=(jax.ShapeDtypeStruct((M, N), BF16), jax.ShapeDtypeStruct((M, N), BF16)),
        grid=(N // tn,),
        in_specs=[
            pl.BlockSpec((M, D), lambda j: (0, 0)),
            pl.BlockSpec((1, D), lambda j: (0, 0)),
            pl.BlockSpec((D, tn), lambda j: (0, j)),
            pl.BlockSpec((D, tn), lambda j: (0, j)),
        ],
        out_specs=(pl.BlockSpec((M, tn), lambda j: (0, j)), pl.BlockSpec((M, tn), lambda j: (0, j))),
        scratch_shapes=[pltpu.VMEM((M, D), BF16)],
        compiler_params=_params("arbitrary"),
        name="kv_proj",
    )(mem, g, wk, wv)


def _mid_kernel(x_ref, ret_ref, lru_ref, wout_ref, g2_ref, wq_ref, k_ref, v_ref, wo_ref, o_ref, att_ref):
    R = ret_ref.shape[1]
    x1 = (x_ref[...]
          + jnp.dot(ret_ref[...], wout_ref[0:R, :], preferred_element_type=F32)
          + jnp.dot(lru_ref[...], wout_ref[R:, :], preferred_element_type=F32))
    xn = _rms(x1, g2_ref[...]).astype(BF16)
    q = jnp.dot(xn, wq_ref[...], preferred_element_type=F32).astype(BF16)
    D = q.shape[1]
    dh = D // XA_HEADS
    scale = dh ** -0.5
    for hd in range(XA_HEADS):
        sl = slice(hd * dh, (hd + 1) * dh)
        s = lax.dot_general(q[:, sl], k_ref[:, sl], (((1,), (1,)), ((), ())),
                            preferred_element_type=F32) * scale
        e = jnp.exp(s - jnp.max(s, axis=-1, keepdims=True))
        p = e / jnp.sum(e, axis=-1, keepdims=True)
        att_ref[:, sl] = jnp.dot(p.astype(BF16), v_ref[:, sl], preferred_element_type=F32).astype(BF16)
    o_ref[...] = x1 + jnp.dot(att_ref[...], wo_ref[...], preferred_element_type=F32)


def _mid(x, ret, lru, w_out, g2, wq, kmem, vmem, wo, B, S, tm):
    T, D = x.shape
    R = ret.shape[1]
    M = kmem.shape[0] // B
    per_b = S // tm
    return pl.pallas_call(
        _mid_kernel,
        out_shape=jax.ShapeDtypeStruct((T, D), F32),
        grid=(T // tm,),
        in_specs=[
            pl.BlockSpec((tm, D), lambda i: (i, 0)),
            pl.BlockSpec((tm, R), lambda i: (i, 0)),
            pl.BlockSpec((tm, R), lambda i: (i, 0)),
            _resident((D, D)),
            pl.BlockSpec((1, D), lambda i: (0, 0)),
            _resident((D, D)),
            pl.BlockSpec((M, D), lambda i: (i // per_b, 0)),
            pl.BlockSpec((M, D), lambda i: (i // per_b, 0)),
            _resident((D, D)),
        ],
        out_specs=pl.BlockSpec((tm, D), lambda i: (i, 0)),
        scratch_shapes=[pltpu.VMEM((tm, D), BF16)],
        compiler_params=_params("arbitrary"),
        name="mid",
    )(x, ret, lru, w_out, g2, wq, kmem, vmem, wo)


def _ffn_kernel(x_ref, g3_ref, wa_ref, wb_ref, cwa_ref, cwb_ref, cba_ref, cbb_ref, wd_ref, gf_ref,
                o_ref, xn_ref, acc_ref, tail_ref, *hbuf_refs, tiles_per_seq):
    i = pl.program_id(0)
    j = pl.program_id(1)
    nj = pl.num_programs(1)
    tm = x_ref.shape[0]
    halo = V7X_SUBLANES
    n_sub = wd_ref.shape[0] // FFN_SUB

    @pl.when(j == 0)
    def _():
        xn_ref[...] = _rms(x_ref[...], g3_ref[...]).astype(BF16)
        acc_ref[...] = jnp.zeros_like(acc_ref)

    @pl.when(jnp.logical_and(j == 0, (i % tiles_per_seq) == 0))
    def _():
        tail_ref[...] = jnp.zeros_like(tail_ref)

    xn = xn_ref[...]

    def up(c, half, w_ref):
        slab = half * n_sub + c
        h = jnp.dot(xn, w_ref[:, c * FFN_SUB:(c + 1) * FFN_SUB], preferred_element_type=F32)
        hbuf_refs[slab][halo:halo + tm, :] = h
        hbuf_refs[slab][0:halo, :] = tail_ref[j, slab]
        tail_ref[j, slab] = h[tm - halo:tm, :]

    def conv(c, half, cw_ref, cb_ref):
        cs = slice(c * FFN_SUB, (c + 1) * FFN_SUB)
        hbuf_ref = hbuf_refs[half * n_sub + c]
        y = cb_ref[:, cs]
        for kk in range(FFN_CONV):
            off = halo - (FFN_CONV - 1) + kk
            y = y + cw_ref[kk:kk + 1, cs] * hbuf_ref[off:off + tm, :]
        return y

    for c in range(n_sub):
        up(c, 0, wa_ref)
        up(c, 1, wb_ref)
    part = None
    for c in range(n_sub):
        ya = conv(c, 0, cwa_ref, cba_ref)
        yb = conv(c, 1, cwb_ref, cbb_ref)
        gated = (ya * _sigmoid(ya) * yb).astype(BF16)
        d = jnp.dot(gated, wd_ref[c * FFN_SUB:(c + 1) * FFN_SUB, :], preferred_element_type=F32)
        part = d if part is None else part + d
    acc_ref[...] += part

    @pl.when(j == nj - 1)
    def _():
        o_ref[...] = _rms(x_ref[...] + acc_ref[...], gf_ref[...])


def _ffn(x, g3, w_up, conv_w, conv_b, w_down, gf, S, tm, tf):
    T, D = x.shape
    F = w_down.shape[0]
    nj = F // tf
    n_slabs = 2 * (tf // FFN_SUB)
    kernel = functools.partial(_ffn_kernel, tiles_per_seq=S // tm)
    return pl.pallas_call(
        kernel,
        out_shape=jax.ShapeDtypeStruct((T, D), F32),
        grid=(T // tm, nj),
        in_specs=[
            pl.BlockSpec((tm, D), lambda i, j: (i, 0)),
            pl.BlockSpec((1, D), lambda i, j: (0, 0)),
            pl.BlockSpec((D, tf), lambda i, j: (0, j)),
            pl.BlockSpec((D, tf), lambda i, j: (0, j + nj)),
            pl.BlockSpec((FFN_CONV, tf), lambda i, j: (0, j)),
            pl.BlockSpec((FFN_CONV, tf), lambda i, j: (0, j + nj)),
            pl.BlockSpec((1, tf), lambda i, j: (0, j)),
            pl.BlockSpec((1, tf), lambda i, j: (0, j + nj)),
            pl.BlockSpec((tf, D), lambda i, j: (j, 0)),
            pl.BlockSpec((1, D), lambda i, j: (0, 0)),
        ],
        out_specs=pl.BlockSpec((tm, D), lambda i, j: (i, 0)),
        scratch_shapes=[
            pltpu.VMEM((tm, D), BF16),
            pltpu.VMEM((tm, D), F32),
            pltpu.VMEM((nj, n_slabs, V7X_SUBLANES, FFN_SUB), F32),
        ] + [pltpu.VMEM((tm + V7X_SUBLANES, FFN_SUB), F32)] * n_slabs,
        compiler_params=_params("arbitrary", "arbitrary"),
        name="ffn",
    )(x, g3, w_up, w_up, conv_w, conv_w, conv_b, conv_b, w_down, gf)


def _layer(x, mem, posf, inv, B, S, norm1_g, w_in, ret_g, rg_conv_w, rg_conv_b, rg_wa, rg_ba, rg_wx, rg_bx,
           rg_lambda, w_out, norm2_g, norm_mem_g, xa_wq, xa_wk, xa_wv, xa_wo, norm3_g, ffn_w_up,
           ffn_conv_w, ffn_conv_b, ffn_w_down, out_g, tiles):
    W = LRU_BLOCKS * LRU_BLOCK_DIM
    row = lambda a: a.reshape(1, -1)
    h = _in_proj(x, row(norm1_g), posf, inv, w_in.astype(BF16), tiles["in_tm"], tiles["ret_chunk"])
    ret = _retention(h, row(ret_g), B, S, tiles["ret_chunk"])
    wg = jnp.concatenate([rg_wa, rg_wx], axis=-1).astype(BF16)
    lru = _rglru(h, rg_conv_w, row(rg_conv_b), wg, rg_ba.reshape(1, W), rg_bx.reshape(1, W),
                 row(rg_lambda), B, S, tiles["lru_tt"])
    kmem, vmem = _kv_proj(mem, row(norm_mem_g), xa_wk.astype(BF16), xa_wv.astype(BF16), tiles["kv_tn"])
    x2 = _mid(x, ret, lru, w_out.astype(BF16), row(norm2_g), xa_wq.astype(BF16), kmem, vmem,
              xa_wo.astype(BF16), B, S, tiles["mid_tm"])
    return _ffn(x2, row(norm3_g), ffn_w_up.astype(BF16), ffn_conv_w, row(ffn_conv_b),
                ffn_w_down.astype(BF16), out_g, S, tiles["ffn_tm"], tiles["ffn_tf"])


_TILES = dict(in_tm=IN_TM, ret_chunk=RET_CHUNK, lru_tt=LRU_TT, kv_tn=KV_TN, mid_tm=MID_TM,
              ffn_tm=FFN_TM, ffn_tf=FFN_TF)


def _block(x, mem, positions, norm1_g, w_in, ret_g, rg_conv_w, rg_conv_b, rg_wa, rg_ba, rg_wx, rg_bx,
           rg_lambda, w_out, norm2_g, norm_mem_g, xa_wq, xa_wk, xa_wv, xa_wo, norm3_g, ffn_w_up,
           ffn_conv_w, ffn_conv_b, ffn_w_down, final_g, tiles):
    B, S, D = x.shape
    assert w_in.shape[0] == 1, "only depth 1 is supported"
    T = B * S
    half = RET_HEAD_DIM // 2
    inv = (ROPE_BASE ** (-jnp.arange(half, dtype=F32) / half)).reshape(1, half)
    posf = jnp.broadcast_to(positions.reshape(T, 1).astype(F32), (T, half))
    out = _layer(x.reshape(T, D), mem.reshape(B * mem.shape[1], D), posf, inv, B, S, norm1_g[0], w_in[0],
                 ret_g[0], rg_conv_w[0], rg_conv_b[0], rg_wa[0], rg_ba[0], rg_wx[0], rg_bx[0], rg_lambda[0],
                 w_out[0], norm2_g[0], norm_mem_g[0], xa_wq[0], xa_wk[0], xa_wv[0], xa_wo[0], norm3_g[0],
                 ffn_w_up[0], ffn_conv_w[0], ffn_conv_b[0], ffn_w_down[0], final_g.reshape(1, D), tiles)
    return out.reshape(B, S, D)


def kernel(x, mem, positions, norm1_g, w_in, ret_g, rg_conv_w, rg_conv_b, rg_wa, rg_ba, rg_wx, rg_bx, rg_lambda, w_out, norm2_g, norm_mem_g, xa_wq, xa_wk, xa_wv, xa_wo, norm3_g, ffn_w_up, ffn_conv_w, ffn_conv_b, ffn_w_down, final_g):
    return _block(x, mem, positions, norm1_g, w_in, ret_g, rg_conv_w, rg_conv_b, rg_wa, rg_ba, rg_wx, rg_bx,
                  rg_lambda, w_out, norm2_g, norm_mem_g, xa_wq, xa_wk, xa_wv, xa_wo, norm3_g, ffn_w_up,
                  ffn_conv_w, ffn_conv_b, ffn_w_down, final_g, _TILES)
```

```python
import functools
import math

import jax
import jax.numpy as jnp
from jax import lax
from jax.experimental import pallas as pl
from jax.experimental.pallas import tpu as pltpu

F32 = jnp.float32
BF16 = jnp.bfloat16

EPS = 1e-6
ROPE_BASE = 10000.0
RET_HEADS = 4
RET_HEAD_DIM = 256
LRU_BLOCKS = 8
LRU_BLOCK_DIM = 128
LRU_CONV = 4
LRU_C = 8.0
XA_HEADS = 4
FFN_CONV = 3

V7X_LANES = 128
V7X_SUBLANES = 8
V7X_VMEM_LIMIT = 56 * 1024 * 1024

IN_TM = 512
RET_CHUNK = 256
LRU_TT = 512
MID_TM = 256
FFN_TM = 512
FFN_TF = 512
FFN_SUB = 256
KV_TN = 512


def _params(*sem):
    return pltpu.CompilerParams(dimension_semantics=sem, vmem_limit_bytes=V7X_VMEM_LIMIT)


def _rms(x, g):
    return x * lax.rsqrt(jnp.mean(x * x, axis=-1, keepdims=True) + EPS) * g


def _sigmoid(x):
    return 1.0 / (1.0 + jnp.exp(-x))


def _resident(shape):
    nd = len(shape)
    return pl.BlockSpec(shape, lambda *_: (0,) * nd, pipeline_mode=pl.Buffered(1))


def _in_proj_kernel(x_ref, g_ref, pos_ref, inv_ref, dec_ref, w_ref, o_ref, xn_ref, *, chunk):
    tm = x_ref.shape[0]
    tn = RET_HEADS * RET_HEAD_DIM
    half = RET_HEAD_DIM // 2
    xn_ref[...] = _rms(x_ref[...], g_ref[...]).astype(BF16)
    xn = xn_ref[...]
    n_groups = o_ref.shape[1] // tn
    cos = sin = None
    for grp in list(range(2, n_groups)) + [0, 1]:
        cs = slice(grp * tn, (grp + 1) * tn)
        h = jnp.dot(xn, w_ref[:, cs], preferred_element_type=F32)
        if cos is None:
            ang = pos_ref[...] * inv_ref[...]
            cos = jnp.cos(ang)
            sin = jnp.sin(ang)
        if grp >= 2:
            o_ref[:, cs] = h.astype(BF16)
            continue
        for r0 in range(0, tm, chunk):
            rs = slice(r0, r0 + chunk)
            for hd in range(RET_HEADS):
                d = dec_ref[grp, :, hd * half:(hd + 1) * half]
                c = cos[rs] * d
                s = sin[rs] * d
                lo = hd * RET_HEAD_DIM
                t1 = h[rs, lo:lo + half]
                t2 = h[rs, lo + half:lo + RET_HEAD_DIM]
                o_ref[rs, grp * tn + lo:grp * tn + lo + half] = (t1 * c - t2 * s).astype(BF16)
                o_ref[rs, grp * tn + lo + half:grp * tn + lo + RET_HEAD_DIM] = (t1 * s + t2 * c).astype(BF16)


def _retention_log_gamma():
    return jnp.log(1.0 - 2.0 ** (-5.0 - jnp.arange(RET_HEADS, dtype=F32)))


def _in_proj(x, g, posf, inv, w, tm, C):
    T, D = x.shape
    N = w.shape[1]
    half = RET_HEAD_DIM // 2
    assert tm % C == 0
    t = jnp.arange(C, dtype=F32)
    log_g = _retention_log_gamma()
    q_dec = jnp.exp(log_g[None, :] * (t[:, None] + 1.0))
    k_dec = jnp.exp(log_g[None, :] * (C - 1.0 - t[:, None])) * (RET_HEAD_DIM ** -0.5)
    dec = jnp.repeat(jnp.stack([q_dec, k_dec]), half, axis=-1)
    return pl.pallas_call(
        functools.partial(_in_proj_kernel, chunk=C),
        out_shape=jax.ShapeDtypeStruct((T, N), BF16),
        grid=(T // tm,),
        in_specs=[
            pl.BlockSpec((tm, D), lambda i: (i, 0)),
            pl.BlockSpec((1, D), lambda i: (0, 0)),
            pl.BlockSpec((tm, half), lambda i: (i, 0)),
            pl.BlockSpec((1, half), lambda i: (0, 0)),
            _resident((2, C, RET_HEADS * half)),
            _resident((D, N)),
        ],
        out_specs=pl.BlockSpec((tm, N), lambda i: (i, 0)),
        scratch_shapes=[pltpu.VMEM((tm, D), BF16)],
        compiler_params=_params("arbitrary"),
        name="in_proj",
    )(x, g, posf, inv, dec, w)


def _retention_kernel(q_ref, k_ref, v_ref, g_ref, mask_ref, cdec_ref, rg_ref, o_ref, state_ref):
    n = pl.program_id(1)

    @pl.when(n == 0)
    def _():
        state_ref[...] = jnp.zeros_like(state_ref)

    dh = RET_HEAD_DIM
    for hd in range(RET_HEADS):
        sl = slice(hd * dh, (hd + 1) * dh)
        q = q_ref[:, sl]
        k = k_ref[:, sl]
        v = v_ref[:, sl]
        state = state_ref[hd]
        s = lax.dot_general(q, k, (((1,), (1,)), ((), ())), preferred_element_type=F32) * mask_ref[hd]
        ret = (jnp.dot(s.astype(BF16), v, preferred_element_type=F32)
               + jnp.dot(q, state.astype(BF16), preferred_element_type=F32))
        state_ref[hd] = state * cdec_ref[hd] + lax.dot_general(
            k, v, (((0,), (0,)), ((), ())), preferred_element_type=F32)
        ret = ret * lax.rsqrt(jnp.mean(ret * ret, axis=-1, keepdims=True) + EPS)
        gate = g_ref[:, sl].astype(F32)
        o_ref[:, sl] = (ret * rg_ref[:, sl] * (gate * _sigmoid(gate))).astype(BF16)


def _retention(h, ret_g, B, S, C):
    T = h.shape[0]
    R = RET_HEADS * RET_HEAD_DIM
    N = S // C
    H, dh = RET_HEADS, RET_HEAD_DIM
    log_g = _retention_log_gamma()
    idx = jnp.arange(C)
    mask = jnp.where(idx[:, None] >= idx[None, :], jnp.exp(-log_g * C)[:, None, None], 0.0)
    c_dec = jnp.broadcast_to(jnp.exp(log_g * C)[:, None, None], (H, 1, dh))

    def col(c):
        return pl.BlockSpec((C, R), lambda b, n: (b * N + n, c))

    return pl.pallas_call(
        _retention_kernel,
        out_shape=jax.ShapeDtypeStruct((T, R), BF16),
        grid=(B, N),
        in_specs=[
            col(0), col(1), col(2), col(3),
            pl.BlockSpec((H, C, C), lambda b, n: (0, 0, 0)),
            pl.BlockSpec((H, 1, dh), lambda b, n: (0, 0, 0)),
            pl.BlockSpec((1, R), lambda b, n: (0, 0)),
        ],
        out_specs=pl.BlockSpec((C, R), lambda b, n: (b * N + n, 0)),
        scratch_shapes=[pltpu.VMEM((H, dh, dh), F32)],
        compiler_params=_params("arbitrary", "arbitrary"),
        name="retention",
    )(h, h, h, h, mask, c_dec, ret_g)


def _rglru_kernel(u_ref, y_ref, cw_ref, cb_ref, wg_ref, ba_ref, bx_ref, lam_ref,
                  o_ref, ubuf_ref, a_ref, b_ref, hcar_ref):
    n = pl.program_id(1)
    tt = u_ref.shape[0]
    W = u_ref.shape[1]
    halo = V7X_SUBLANES

    @pl.when(n == 0)
    def _():
        ubuf_ref[0:halo, :] = jnp.zeros((halo, W), F32)
        hcar_ref[...] = jnp.zeros_like(hcar_ref)

    @pl.when(n > 0)
    def _():
        ubuf_ref[0:halo, :] = ubuf_ref[tt:tt + halo, :]

    ubuf_ref[halo:halo + tt, :] = u_ref[...].astype(F32)

    uc = cb_ref[...] + cw_ref[LRU_CONV - 1:LRU_CONV, :] * ubuf_ref[halo:halo + tt, :]
    for kk in range(LRU_CONV - 1):
        off = halo - (LRU_CONV - 1) + kk
        uc = uc + cw_ref[kk:kk + 1, :] * ubuf_ref[off:off + tt, :]

    lam = lam_ref[...]
    log_sig = jnp.minimum(lam, 0.0) - jnp.log1p(jnp.exp(-jnp.abs(lam)))
    bd = LRU_BLOCK_DIM
    for nb in range(LRU_BLOCKS):
        sl = slice(nb * bd, (nb + 1) * bd)
        ub = uc[:, sl]
        z = jnp.dot(ub.astype(BF16), wg_ref[nb], preferred_element_type=F32)
        r = _sigmoid(z[:, :bd] + ba_ref[:, sl])
        ig = _sigmoid(z[:, bd:] + bx_ref[:, sl])
        log_a = LRU_C * r * log_sig[:, sl]
        a_ref[:, sl] = jnp.exp(log_a)
        th = jnp.tanh(log_a)
        b_ref[:, sl] = jnp.sqrt(-2.0 * th / (1.0 - th)) * (ig * ub)

    row = lax.broadcasted_iota(jnp.int32, (V7X_SUBLANES, W), 0)

    def group(gi, hprev):
        r0 = pl.multiple_of(gi * V7X_SUBLANES, V7X_SUBLANES)
        a = a_ref[pl.ds(r0, V7X_SUBLANES), :]
        b = b_ref[pl.ds(r0, V7X_SUBLANES), :]
        for sh in (1, 2, 4):
            keep = row >= sh
            a_sh = jnp.where(keep, pltpu.roll(a, sh, 0), 1.0)
            b_sh = jnp.where(keep, pltpu.roll(b, sh, 0), 0.0)
            b = a * b_sh + b
            a = a * a_sh
        hcur = b + a * hprev
        b_ref[pl.ds(r0, V7X_SUBLANES), :] = hcur
        return jnp.broadcast_to(hcur[V7X_SUBLANES - 1:V7X_SUBLANES, :], (V7X_SUBLANES, W))

    hlast = lax.fori_loop(0, tt // V7X_SUBLANES, group, hcar_ref[...], unroll=4)
    hcar_ref[...] = hlast

    y = y_ref[...].astype(F32)
    gelu = 0.5 * y * (1.0 + jnp.tanh(math.sqrt(2.0 / math.pi) * (y + 0.044715 * (y * y * y))))
    o_ref[...] = (b_ref[...] * gelu).astype(BF16)


def _rglru(h, conv_w, conv_b, wg, ba, bx, lam, B, S, tt):
    T = h.shape[0]
    W = LRU_BLOCKS * LRU_BLOCK_DIM
    N = S // tt

    def const(shape):
        nd = len(shape)
        return pl.BlockSpec(shape, lambda b, n: (0,) * nd)

    return pl.pallas_call(
        _rglru_kernel,
        out_shape=jax.ShapeDtypeStruct((T, W), BF16),
        grid=(B, N),
        in_specs=[
            pl.BlockSpec((tt, W), lambda b, n: (b * N + n, 4)),
            pl.BlockSpec((tt, W), lambda b, n: (b * N + n, 5)),
            const((LRU_CONV, W)), const((1, W)),
            const((LRU_BLOCKS, LRU_BLOCK_DIM, 2 * LRU_BLOCK_DIM)),
            const((1, W)), const((1, W)), const((1, W)),
        ],
        out_specs=pl.BlockSpec((tt, W), lambda b, n: (b * N + n, 0)),
        scratch_shapes=[
            pltpu.VMEM((tt + V7X_SUBLANES, W), F32),
            pltpu.VMEM((tt, W), F32),
            pltpu.VMEM((tt, W), F32),
            pltpu.VMEM((V7X_SUBLANES, W), F32),
        ],
        compiler_params=_params("arbitrary", "arbitrary"),
        name="rglru",
    )(h, h, conv_w, conv_b, wg, ba, bx, lam)


def _kv_proj_kernel(m_ref, g_ref, wk_ref, wv_ref, k_ref, v_ref, mn_ref):
    @pl.when(pl.program_id(0) == 0)
    def _():
        mn_ref[...] = _rms(m_ref[...], g_ref[...]).astype(BF16)

    mn = mn_ref[...]
    k_ref[...] = jnp.dot(mn, wk_ref[...], preferred_element_type=F32).astype(BF16)
    v_ref[...] = jnp.dot(mn, wv_ref[...], preferred_element_type=F32).astype(BF16)


def _kv_proj(mem, g, wk, wv, tn):
    M, D = mem.shape
    N = wk.shape[1]
    return pl.pallas_call(
        _kv_proj_kernel,
        out_shape=(jax.ShapeDtypeStruct((M, N), BF16), jax.ShapeDtypeStruct((M, N), BF16)),
        grid=(N // tn,),
        in_specs=[
            pl.BlockSpec((M, D), lambda j: (0, 0)),
            pl.BlockSpec((1, D), lambda j: (0, 0)),
            pl.BlockSpec((D, tn), lambda j: (0, j)),
            pl.BlockSpec((D, tn), lambda j: (0, j)),
        ],
        out_specs=(pl.BlockSpec((M, tn), lambda j: (0, j)), pl.BlockSpec((M, tn), lambda j: (0, j))),
        scratch_shapes=[pltpu.VMEM((M, D), BF16)],
        compiler_params=_params("arbitrary"),
        name="kv_proj",
    )(mem, g, wk, wv)


def _mid_kernel(x_ref, ret_ref, lru_ref, wout_ref, g2_ref, wq_ref, k_ref, v_ref, wo_ref, o_ref, att_ref):
    R = ret_ref.shape[1]
    x1 = (x_ref[...]
          + jnp.dot(ret_ref[...], wout_ref[0:R, :], preferred_element_type=F32)
          + jnp.dot(lru_ref[...], wout_ref[R:, :], preferred_element_type=F32))
    xn = _rms(x1, g2_ref[...]).astype(BF16)
    q = jnp.dot(xn, wq_ref[...], preferred_element_type=F32).astype(BF16)
    D = q.shape[1]
    dh = D // XA_HEADS
    scale = dh ** -0.5
    for hd in range(XA_HEADS):
        sl = slice(hd * dh, (hd + 1) * dh)
        s = lax.dot_general(q[:, sl], k_ref[:, sl], (((1,), (1,)), ((), ())),
                            preferred_element_type=F32) * scale
        e = jnp.exp(s - jnp.max(s, axis=-1, keepdims=True))
        p = e / jnp.sum(e, axis=-1, keepdims=True)
        att_ref[:, sl] = jnp.dot(p.astype(BF16), v_ref[:, sl], preferred_element_type=F32).astype(BF16)
    o_ref[...] = x1 + jnp.dot(att_ref[...], wo_ref[...], preferred_element_type=F32)


def _mid(x, ret, lru, w_out, g2, wq, kmem, vmem, wo, B, S, tm):
    T, D = x.shape
    R = ret.shape[1]
    M = kmem.shape[0] // B
    per_b = S // tm
    return pl.pallas_call(
        _mid_kernel,
        out_shape=jax.ShapeDtypeStruct((T, D), F32),
        grid=(T // tm,),
        in_specs=[
            pl.BlockSpec((tm, D), lambda i: (i, 0)),
            pl.BlockSpec((tm, R), lambda i: (i, 0)),
            pl.BlockSpec((tm, R), lambda i: (i, 0)),
            _resident((D, D)),
            pl.BlockSpec((1, D), lambda i: (0, 0)),
            _resident((D, D)),
            pl.BlockSpec((M, D), lambda i: (i // per_b, 0)),
            pl.BlockSpec((M, D), lambda i: (i // per_b, 0)),
            _resident((D, D)),
        ],
        out_specs=pl.BlockSpec((tm, D), lambda i: (i, 0)),
        scratch_shapes=[pltpu.VMEM((tm, D), BF16)],
        compiler_params=_params("arbitrary"),
        name="mid",
    )(x, ret, lru, w_out, g2, wq, kmem, vmem, wo)


def _ffn_kernel(x_ref, g3_ref, wa_ref, wb_ref, cwa_ref, cwb_ref, cba_ref, cbb_ref, wd_ref, gf_ref,
                o_ref, xn_ref, acc_ref, tail_ref, *hbuf_refs, tiles_per_seq):
    i = pl.program_id(0)
    j = pl.program_id(1)
    nj = pl.num_programs(1)
    tm = x_ref.shape[0]
    halo = V7X_SUBLANES
    n_sub = wd_ref.shape[0] // FFN_SUB

    @pl.when(j == 0)
    def _():
        xn_ref[...] = _rms(x_ref[...], g3_ref[...]).astype(BF16)
        acc_ref[...] = jnp.zeros_like(acc_ref)

    @pl.when(jnp.logical_and(j == 0, (i % tiles_per_seq) == 0))
    def _():
        tail_ref[...] = jnp.zeros_like(tail_ref)

    xn = xn_ref[...]

    def up(c, half, w_ref):
        slab = half * n_sub + c
        h = jnp.dot(xn, w_ref[:, c * FFN_SUB:(c + 1) * FFN_SUB], preferred_element_type=F32)
        hbuf_refs[slab][halo:halo + tm, :] = h
        hbuf_refs[slab][0:halo, :] = tail_ref[j, slab]
        tail_ref[j, slab] = h[tm - halo:tm, :]

    def conv(c, half, cw_ref, cb_ref):
        cs = slice(c * FFN_SUB, (c + 1) * FFN_SUB)
        hbuf_ref = hbuf_refs[half * n_sub + c]
        y = cb_ref[:, cs]
        for kk in range(FFN_CONV):
            off = halo - (FFN_CONV - 1) + kk
            y = y + cw_ref[kk:kk + 1, cs] * hbuf_ref[off:off + tm, :]
        return y

    for c in range(n_sub):
        up(c, 0, wa_ref)
        up(c, 1, wb_ref)
    part = None
    for c in range(n_sub):
        ya = conv(c, 0, cwa_ref, cba_ref)
        yb = conv(c, 1, cwb_ref, cbb_ref)
        gated = (ya * _sigmoid(ya) * yb).astype(BF16)
        d = jnp.dot(gated, wd_ref[c * FFN_SUB:(c + 1) * FFN_SUB, :], preferred_element_type=F32)
        part = d if part is None else part + d
    acc_ref[...] += part

    @pl.when(j == nj - 1)
    def _():
        o_ref[...] = _rms(x_ref[...] + acc_ref[...], gf_ref[...])


def _ffn(x, g3, w_up, conv_w, conv_b, w_down, gf, S, tm, tf):
    T, D = x.shape
    F = w_down.shape[0]
    nj = F // tf
    n_slabs = 2 * (tf // FFN_SUB)
    kernel = functools.partial(_ffn_kernel, tiles_per_seq=S // tm)
    return pl.pallas_call(
        kernel,
        out_shape=jax.ShapeDtypeStruct((T, D), F32),
        grid=(T // tm, nj),
        in_specs=[
            pl.BlockSpec((tm, D), lambda i, j: (i, 0)),
            pl.BlockSpec((1, D), lambda i, j: (0, 0)),
            pl.BlockSpec((D, tf), lambda i, j: (0, j)),
            pl.BlockSpec((D, tf), lambda i, j: (0, j + nj)),
            pl.BlockSpec((FFN_CONV, tf), lambda i, j: (0, j)),
            pl.BlockSpec((FFN_CONV, tf), lambda i, j: (0, j + nj)),
            pl.BlockSpec((1, tf), lambda i, j: (0, j)),
            pl.BlockSpec((1, tf), lambda i, j: (0, j + nj)),
            pl.BlockSpec((tf, D), lambda i, j: (j, 0)),
            pl.BlockSpec((1, D), lambda i, j: (0, 0)),
        ],
        out_specs=pl.BlockSpec((tm, D), lambda i, j: (i, 0)),
        scratch_shapes=[
            pltpu.VMEM((tm, D), BF16),
            pltpu.VMEM((tm, D), F32),
            pltpu.VMEM((nj, n_slabs, V7X_SUBLANES, FFN_SUB), F32),
        ] + [pltpu.VMEM((tm + V7X_SUBLANES, FFN_SUB), F32)] * n_slabs,
        compiler_params=_params("arbitrary", "arbitrary"),
        name="ffn",
    )(x, g3, w_up, w_up, conv_w, conv_w, conv_b, conv_b, w_down, gf)


def _layer(x, mem, posf, inv, B, S, norm1_g, w_in, ret_g, rg_conv_w, rg_conv_b, rg_wa, rg_ba, rg_wx, rg_bx,
           rg_lambda, w_out, norm2_g, norm_mem_g, xa_wq, xa_wk, xa_wv, xa_wo, norm3_g, ffn_w_up,
           ffn_conv_w, ffn_conv_b, ffn_w_down, out_g, tiles):
    W = LRU_BLOCKS * LRU_BLOCK_DIM
    row = lambda a: a.reshape(1, -1)
    h = _in_proj(x, row(norm1_g), posf, inv, w_in.astype(BF16), tiles["in_tm"], tiles["ret_chunk"])
    ret = _retention(h, row(ret_g), B, S, tiles["ret_chunk"])
    wg = jnp.concatenate([rg_wa, rg_wx], axis=-1).astype(BF16)
    lru = _rglru(h, rg_conv_w, row(rg_conv_b), wg, rg_ba.reshape(1, W), rg_bx.reshape(1, W),
                 row(rg_lambda), B, S, tiles["lru_tt"])
    kmem, vmem = _kv_proj(mem, row(norm_mem_g), xa_wk.astype(BF16), xa_wv.astype(BF16), tiles["kv_tn"])
    x2 = _mid(x, ret, lru, w_out.astype(BF16), row(norm2_g), xa_wq.astype(BF16), kmem, vmem,
              xa_wo.astype(BF16), B, S, tiles["mid_tm"])
    return _ffn(x2, row(norm3_g), ffn_w_up.astype(BF16), ffn_conv_w, row(ffn_conv_b),
                ffn_w_down.astype(BF16), out_g, S, tiles["ffn_tm"], tiles["ffn_tf"])


_TILES = dict(in_tm=IN_TM, ret_chunk=RET_CHUNK, lru_tt=LRU_TT, kv_tn=KV_TN, mid_tm=MID_TM,
              ffn_tm=FFN_TM, ffn_tf=FFN_TF)


def _block(x, mem, positions, norm1_g, w_in, ret_g, rg_conv_w, rg_conv_b, rg_wa, rg_ba, rg_wx, rg_bx,
           rg_lambda, w_out, norm2_g, norm_mem_g, xa_wq, xa_wk, xa_wv, xa_wo, norm3_g, ffn_w_up,
           ffn_conv_w, ffn_conv_b, ffn_w_down, final_g, tiles):
    B, S, D = x.shape
    assert w_in.shape[0] == 1, "only depth 1 is supported"
    T = B * S
    half = RET_HEAD_DIM // 2
    inv = (ROPE_BASE ** (-jnp.arange(half, dtype=F32) / half)).reshape(1, half)
    posf = jnp.broadcast_to(positions.reshape(T, 1).astype(F32), (T, half))
    out = _layer(x.reshape(T, D), mem.reshape(B * mem.shape[1], D), posf, inv, B, S, norm1_g[0], w_in[0],
                 ret_g[0], rg_conv_w[0], rg_conv_b[0], rg_wa[0], rg_ba[0], rg_wx[0], rg_bx[0], rg_lambda[0],
                 w_out[0], norm2_g[0], norm_mem_g[0], xa_wq[0], xa_wk[0], xa_wv[0], xa_wo[0], norm3_g[0],
                 ffn_w_up[0], ffn_conv_w[0], ffn_conv_b[0], ffn_w_down[0], final_g.reshape(1, D), tiles)
    return out.reshape(B, S, D)


def kernel(x, mem, positions, norm1_g, w_in, ret_g, rg_conv_w, rg_conv_b, rg_wa, rg_ba, rg_wx, rg_bx, rg_lambda, w_out, norm2_g, norm_mem_g, xa_wq, xa_wk, xa_wv, xa_wo, norm3_g, ffn_w_up, ffn_conv_w, ffn_conv_b, ffn_w_down, final_g):
    return _block(x, mem, positions, norm1_g, w_in, ret_g, rg_conv_w, rg_conv_b, rg_wa, rg_ba, rg_wx, rg_bx,
                  rg_lambda, w_out, norm2_g, norm_mem_g, xa_wq, xa_wk, xa_wv, xa_wo, norm3_g, ffn_w_up,
                  ffn_conv_w, ffn_conv_b, ffn_w_down, final_g, _TILES)
```

```python
import functools
import math

import jax
import jax.numpy as jnp
from jax import lax
from jax.experimental import pallas as pl
from jax.experimental.pallas import tpu as pltpu

F32 = jnp.float32
BF16 = jnp.bfloat16

EPS = 1e-6
ROPE_BASE = 10000.0
RET_HEADS = 4
RET_HEAD_DIM = 256
LRU_BLOCKS = 8
LRU_BLOCK_DIM = 128
LRU_CONV = 4
LRU_C = 8.0
XA_HEADS = 4
FFN_CONV = 3

V7X_LANES = 128
V7X_SUBLANES = 8
V7X_F32_TINY = 1.1754944e-38
V7X_VMEM_LIMIT = 56 * 1024 * 1024

IN_TM = 512
RET_CHUNK = 256
LRU_TT = 512
MID_TM = 256
FFN_TM = 512
FFN_TF = 512
FFN_SUB = 256
FFN_RB = 128
KV_TN = 512


def _params(*sem):
    return pltpu.CompilerParams(dimension_semantics=sem, vmem_limit_bytes=V7X_VMEM_LIMIT)


def _rms(x, g):
    return x * lax.rsqrt(jnp.mean(x * x, axis=-1, keepdims=True) + EPS) * g


def _sigmoid(x):
    return 1.0 / (1.0 + jnp.exp(-x))


def _resident(shape):
    nd = len(shape)
    return pl.BlockSpec(shape, lambda *_: (0,) * nd, pipeline_mode=pl.Buffered(1))


def _in_proj_kernel(x_ref, g_ref, pos_ref, inv_ref, dec_ref, w_ref, o_ref, xn_ref, *, chunk):
    tm = x_ref.shape[0]
    tn = RET_HEADS * RET_HEAD_DIM
    half = RET_HEAD_DIM // 2
    xn_ref[...] = _rms(x_ref[...], g_ref[...]).astype(BF16)
    xn = xn_ref[...]
    n_groups = o_ref.shape[1] // tn
    cos = sin = None
    for grp in list(range(2, n_groups)) + [0, 1]:
        cs = slice(grp * tn, (grp + 1) * tn)
        h = jnp.dot(xn, w_ref[:, cs], preferred_element_type=F32)
        if cos is None:
            ang = pos_ref[...] * inv_ref[...]
            cos = jnp.cos(ang)
            sin = jnp.sin(ang)
        if grp >= 2:
            o_ref[:, cs] = h.astype(BF16)
            continue
        for r0 in range(0, tm, chunk):
            rs = slice(r0, r0 + chunk)
            for hd in range(RET_HEADS):
                d = dec_ref[grp, :, hd * half:(hd + 1) * half]
                c = cos[rs] * d
                s = sin[rs] * d
                lo = hd * RET_HEAD_DIM
                t1 = h[rs, lo:lo + half]
                t2 = h[rs, lo + half:lo + RET_HEAD_DIM]
                o_ref[rs, grp * tn + lo:grp * tn + lo + half] = (t1 * c - t2 * s).astype(BF16)
                o_ref[rs, grp * tn + lo + half:grp * tn + lo + RET_HEAD_DIM] = (t1 * s + t2 * c).astype(BF16)


def _retention_log_gamma():
    return jnp.log(1.0 - 2.0 ** (-5.0 - jnp.arange(RET_HEADS, dtype=F32)))


def _in_proj(x, g, posf, inv, w, tm, C):
    T, D = x.shape
    N = w.shape[1]
    half = RET_HEAD_DIM // 2
    assert tm % C == 0
    t = jnp.arange(C, dtype=F32)
    log_g = _retention_log_gamma()
    q_dec = jnp.exp(log_g[None, :] * (t[:, None] + 1.0))
    k_dec = jnp.exp(log_g[None, :] * (C - 1.0 - t[:, None])) * (RET_HEAD_DIM ** -0.5)
    dec = jnp.repeat(jnp.stack([q_dec, k_dec]), half, axis=-1)
    return pl.pallas_call(
        functools.partial(_in_proj_kernel, chunk=C),
        out_shape=jax.ShapeDtypeStruct((T, N), BF16),
        grid=(T // tm,),
        in_specs=[
            pl.BlockSpec((tm, D), lambda i: (i, 0)),
            pl.BlockSpec((1, D), lambda i: (0, 0)),
            pl.BlockSpec((tm, half), lambda i: (i, 0)),
            pl.BlockSpec((1, half), lambda i: (0, 0)),
            _resident((2, C, RET_HEADS * half)),
            _resident((D, N)),
        ],
        out_specs=pl.BlockSpec((tm, N), lambda i: (i, 0)),
        scratch_shapes=[pltpu.VMEM((tm, D), BF16)],
        compiler_params=_params("arbitrary"),
        name="in_proj",
    )(x, g, posf, inv, dec, w)


def _retention_kernel(q_ref, k_ref, v_ref, g_ref, mask_ref, cdec_ref, rg_ref, o_ref, state_ref):
    n = pl.program_id(1)

    @pl.when(n == 0)
    def _():
        state_ref[...] = jnp.zeros_like(state_ref)

    dh = RET_HEAD_DIM
    for hd in range(RET_HEADS):
        sl = slice(hd * dh, (hd + 1) * dh)
        q = q_ref[:, sl]
        k = k_ref[:, sl]
        v = v_ref[:, sl]
        state = state_ref[hd]
        s = lax.dot_general(q, k, (((1,), (1,)), ((), ())), preferred_element_type=F32) * mask_ref[hd]
        ret = (jnp.dot(s.astype(BF16), v, preferred_element_type=F32)
               + jnp.dot(q, state.astype(BF16), preferred_element_type=F32))
        state_ref[hd] = state * cdec_ref[hd] + lax.dot_general(
            k, v, (((0,), (0,)), ((), ())), preferred_element_type=F32)
        ret = ret * lax.rsqrt(jnp.mean(ret * ret, axis=-1, keepdims=True) + EPS)
        gate = g_ref[:, sl].astype(F32)
        o_ref[:, sl] = (ret * rg_ref[:, sl] * (gate * _sigmoid(gate))).astype(BF16)


def _retention(h, ret_g, B, S, C):
    T = h.shape[0]
    R = RET_HEADS * RET_HEAD_DIM
    N = S // C
    H, dh = RET_HEADS, RET_HEAD_DIM
    log_g = _retention_log_gamma()
    idx = jnp.arange(C)
    mask = jnp.where(idx[:, None] >= idx[None, :], jnp.exp(-log_g * C)[:, None, None], 0.0)
    c_dec = jnp.broadcast_to(jnp.exp(log_g * C)[:, None, None], (H, 1, dh))

    def col(c):
        return pl.BlockSpec((C, R), lambda b, n: (b * N + n, c))

    return pl.pallas_call(
        _retention_kernel,
        out_shape=jax.ShapeDtypeStruct((T, R), BF16),
        grid=(B, N),
        in_specs=[
            col(0), col(1), col(2), col(3),
            pl.BlockSpec((H, C, C), lambda b, n: (0, 0, 0)),
            pl.BlockSpec((H, 1, dh), lambda b, n: (0, 0, 0)),
            pl.BlockSpec((1, R), lambda b, n: (0, 0)),
        ],
        out_specs=pl.BlockSpec((C, R), lambda b, n: (b * N + n, 0)),
        scratch_shapes=[pltpu.VMEM((H, dh, dh), F32)],
        compiler_params=_params("arbitrary", "arbitrary"),
        name="retention",
    )(h, h, h, h, mask, c_dec, ret_g)


def _rglru_kernel(u_ref, y_ref, cw_ref, cb_ref, wg_ref, ba_ref, bx_ref, lam_ref,
                  o_ref, ubuf_ref, a_ref, b_ref, hcar_ref):
    n = pl.program_id(1)
    tt = u_ref.shape[0]
    W = u_ref.shape[1]
    halo = V7X_SUBLANES

    @pl.when(n == 0)
    def _():
        ubuf_ref[0:halo, :] = jnp.zeros((halo, W), F32)
        hcar_ref[...] = jnp.zeros_like(hcar_ref)

    @pl.when(n > 0)
    def _():
        ubuf_ref[0:halo, :] = ubuf_ref[tt:tt + halo, :]

    ubuf_ref[halo:halo + tt, :] = u_ref[...].astype(F32)

    uc = cb_ref[...] + cw_ref[LRU_CONV - 1:LRU_CONV, :] * ubuf_ref[halo:halo + tt, :]
    for kk in range(LRU_CONV - 1):
        off = halo - (LRU_CONV - 1) + kk
        uc = uc + cw_ref[kk:kk + 1, :] * ubuf_ref[off:off + tt, :]

    lam = lam_ref[...]
    log_sig = jnp.minimum(lam, 0.0) - jnp.log1p(jnp.exp(-jnp.abs(lam)))
    bd = LRU_BLOCK_DIM
    for nb in range(LRU_BLOCKS):
        sl = slice(nb * bd, (nb + 1) * bd)
        ub = uc[:, sl]
        z = jnp.dot(ub.astype(BF16), wg_ref[nb], preferred_element_type=F32)
        r = _sigmoid(z[:, :bd] + ba_ref[:, sl])
        ig = _sigmoid(z[:, bd:] + bx_ref[:, sl])
        log_a = LRU_C * r * log_sig[:, sl]
        a_ref[:, sl] = jnp.exp(log_a)
        th = jnp.tanh(log_a)
        q = -2.0 * th / (1.0 - th)
        b_ref[:, sl] = q * lax.rsqrt(jnp.maximum(q, V7X_F32_TINY)) * (ig * ub)

    row = lax.broadcasted_iota(jnp.int32, (V7X_SUBLANES, W), 0)

    def group(gi, hprev):
        r0 = pl.multiple_of(gi * V7X_SUBLANES, V7X_SUBLANES)
        a = a_ref[pl.ds(r0, V7X_SUBLANES), :]
        b = b_ref[pl.ds(r0, V7X_SUBLANES), :]
        for sh in (1, 2, 4):
            keep = row >= sh
            a_sh = jnp.where(keep, pltpu.roll(a, sh, 0), 1.0)
            b_sh = jnp.where(keep, pltpu.roll(b, sh, 0), 0.0)
            b = a * b_sh + b
            a = a * a_sh
        hcur = b + a * hprev
        b_ref[pl.ds(r0, V7X_SUBLANES), :] = hcur
        return jnp.broadcast_to(hcur[V7X_SUBLANES - 1:V7X_SUBLANES, :], (V7X_SUBLANES, W))

    hlast = lax.fori_loop(0, tt // V7X_SUBLANES, group, hcar_ref[...], unroll=4)
    hcar_ref[...] = hlast

    y = y_ref[...].astype(F32)
    gelu = 0.5 * y * (1.0 + jnp.tanh(math.sqrt(2.0 / math.pi) * (y + 0.044715 * (y * y * y))))
    o_ref[...] = (b_ref[...] * gelu).astype(BF16)


def _rglru(h, conv_w, conv_b, wg, ba, bx, lam, B, S, tt):
    T = h.shape[0]
    W = LRU_BLOCKS * LRU_BLOCK_DIM
    N = S // tt

    def const(shape):
        nd = len(shape)
        return pl.BlockSpec(shape, lambda b, n: (0,) * nd)

    return pl.pallas_call(
        _rglru_kernel,
        out_shape=jax.ShapeDtypeStruct((T, W), BF16),
        grid=(B, N),
        in_specs=[
            pl.BlockSpec((tt, W), lambda b, n: (b * N + n, 4)),
            pl.BlockSpec((tt, W), lambda b, n: (b * N + n, 5)),
            const((LRU_CONV, W)), const((1, W)),
            const((LRU_BLOCKS, LRU_BLOCK_DIM, 2 * LRU_BLOCK_DIM)),
            const((1, W)), const((1, W)), const((1, W)),
        ],
        out_specs=pl.BlockSpec((tt, W), lambda b, n: (b * N + n, 0)),
        scratch_shapes=[
            pltpu.VMEM((tt + V7X_SUBLANES, W), F32),
            pltpu.VMEM((tt, W), F32),
            pltpu.VMEM((tt, W), F32),
            pltpu.VMEM((V7X_SUBLANES, W), F32),
        ],
        compiler_params=_params("arbitrary", "arbitrary"),
        name="rglru",
    )(h, h, conv_w, conv_b, wg, ba, bx, lam)


def _kv_proj_kernel(m_ref, g_ref, wk_ref, wv_ref, k_ref, v_ref, mn_ref):
    @pl.when(pl.program_id(0) == 0)
    def _():
        mn_ref[...] = _rms(m_ref[...], g_ref[...]).astype(BF16)

    mn = mn_ref[...]
    k_ref[...] = jnp.dot(mn, wk_ref[...], preferred_element_type=F32).astype(BF16)
    v_ref[...] = jnp.dot(mn, wv_ref[...], preferred_element_type=F32).astype(BF16)


def _kv_proj(mem, g, wk, wv, tn):
    M, D = mem.shape
    N = wk.shape[1]
    return pl.pallas_call(
        _kv_proj_kernel,
        out_shape=(jax.ShapeDtypeStruct((M, N), BF16), jax.ShapeDtypeStruct((M, N), BF16)),
        grid=(N // tn,),
        in_specs=[
            pl.BlockSpec((M, D), lambda j: (0, 0)),
            pl.BlockSpec((1, D), lambda j: (0, 0)),
            pl.BlockSpec((D, tn), lambda j: (0, j)),
            pl.BlockSpec((D, tn), lambda j: (0, j)),
        ],
        out_specs=(pl.BlockSpec((M, tn), lambda j: (0, j)), pl.BlockSpec((M, tn), lambda j: (0, j))),
        scratch_shapes=[pltpu.VMEM((M, D), BF16)],
        compiler_params=_params("arbitrary"),
        name="kv_proj",
    )(mem, g, wk, wv)


def _mid_kernel(x_ref, ret_ref, lru_ref, wout_ref, g2_ref, wq_ref, k_ref, v_ref, wo_ref, o_ref, att_ref):
    R = ret_ref.shape[1]
    x1 = (x_ref[...]
          + jnp.dot(ret_ref[...], wout_ref[0:R, :], preferred_element_type=F32)
          + jnp.dot(lru_ref[...], wout_ref[R:, :], preferred_element_type=F32))
    xn = _rms(x1, g2_ref[...]).astype(BF16)
    q = jnp.dot(xn, wq_ref[...], preferred_element_type=F32).astype(BF16)
    D = q.shape[1]
    dh = D // XA_HEADS
    scale = dh ** -0.5
    for hd in range(XA_HEADS):
        sl = slice(hd * dh, (hd + 1) * dh)
        s = lax.dot_general(q[:, sl], k_ref[:, sl], (((1,), (1,)), ((), ())),
                            preferred_element_type=F32) * scale
        e = jnp.exp(s - jnp.max(s, axis=-1, keepdims=True))
        p = e / jnp.sum(e, axis=-1, keepdims=True)
        att_ref[:, sl] = jnp.dot(p.astype(BF16), v_ref[:, sl], preferred_element_type=F32).astype(BF16)
    o_ref[...] = x1 + jnp.dot(att_ref[...], wo_ref[...], preferred_element_type=F32)


def _mid(x, ret, lru, w_out, g2, wq, kmem, vmem, wo, B, S, tm):
    T, D = x.shape
    R = ret.shape[1]
    M = kmem.shape[0] // B
    per_b = S // tm
    return pl.pallas_call(
        _mid_kernel,
        out_shape=jax.ShapeDtypeStruct((T, D), F32),
        grid=(T // tm,),
        in_specs=[
            pl.BlockSpec((tm, D), lambda i: (i, 0)),
            pl.BlockSpec((tm, R), lambda i: (i, 0)),
            pl.BlockSpec((tm, R), lambda i: (i, 0)),
            _resident((D, D)),
            pl.BlockSpec((1, D), lambda i: (0, 0)),
            _resident((D, D)),
            pl.BlockSpec((M, D), lambda i: (i // per_b, 0)),
            pl.BlockSpec((M, D), lambda i: (i // per_b, 0)),
            _resident((D, D)),
        ],
        out_specs=pl.BlockSpec((tm, D), lambda i: (i, 0)),
        scratch_shapes=[pltpu.VMEM((tm, D), BF16)],
        compiler_params=_params("arbitrary"),
        name="mid",
    )(x, ret, lru, w_out, g2, wq, kmem, vmem, wo)


def _ffn_kernel(x_ref, g3_ref, wa_ref, wb_ref, cwa_ref, cwb_ref, cba_ref, cbb_ref, wd_ref, gf_ref,
                o_ref, xn_ref, acc_ref, tail_ref, *scratch, tiles_per_seq):
    i = pl.program_id(0)
    j = pl.program_id(1)
    nj = pl.num_programs(1)
    tm = x_ref.shape[0]
    halo = V7X_SUBLANES
    n_sub = wd_ref.shape[0] // FFN_SUB
    hbuf_refs = scratch[:2 * n_sub]
    gate_refs = scratch[2 * n_sub:]
    n_rb = tm // FFN_RB

    @pl.when(j == 0)
    def _():
        xn_ref[...] = _rms(x_ref[...], g3_ref[...]).astype(BF16)
        acc_ref[...] = jnp.zeros_like(acc_ref)

    @pl.when(jnp.logical_and(j == 0, (i % tiles_per_seq) == 0))
    def _():
        tail_ref[...] = jnp.zeros_like(tail_ref)

    def cols(c):
        return slice(c * FFN_SUB, (c + 1) * FFN_SUB)

    def rows(rb):
        return slice(rb * FFN_RB, (rb + 1) * FFN_RB)

    for slab in range(2 * n_sub):
        hbuf_refs[slab][0:halo, :] = tail_ref[j, slab]

    def up(rb):
        lhs = xn_ref[rows(rb), :]
        for c in range(n_sub):
            for half, w_ref in ((0, wa_ref), (1, wb_ref)):
                h = jnp.dot(lhs, w_ref[:, cols(c)], preferred_element_type=F32)
                hbuf_refs[half * n_sub + c][halo + rb * FFN_RB:halo + (rb + 1) * FFN_RB, :] = h

    def conv(c, half, cw_ref, cb_ref, rb):
        hbuf_ref = hbuf_refs[half * n_sub + c]
        y = cb_ref[:, cols(c)]
        for kk in range(FFN_CONV):
            off = halo - (FFN_CONV - 1) + kk + rb * FFN_RB
            y = y + cw_ref[kk:kk + 1, cols(c)] * hbuf_ref[off:off + FFN_RB, :]
        return y

    def gate(rb):
        for c in range(n_sub):
            ya = conv(c, 0, cwa_ref, cba_ref, rb)
            yb = conv(c, 1, cwb_ref, cbb_ref, rb)
            gate_refs[c][rows(rb), :] = (ya * _sigmoid(ya) * yb).astype(BF16)

    def down(rb):
        d = None
        for c in range(n_sub):
            p = jnp.dot(gate_refs[c][rows(rb), :], wd_ref[cols(c), :], preferred_element_type=F32)
            d = p if d is None else d + p
        acc_ref[rows(rb), :] += d

    up(0)
    for rb in range(n_rb):
        if rb + 1 < n_rb:
            up(rb + 1)
        gate(rb)
        down(rb)
    for slab in range(2 * n_sub):
        tail_ref[j, slab] = hbuf_refs[slab][tm:tm + halo, :]

    @pl.when(j == nj - 1)
    def _():
        o_ref[...] = _rms(x_ref[...] + acc_ref[...], gf_ref[...])


def _ffn(x, g3, w_up, conv_w, conv_b, w_down, gf, S, tm, tf):
    T, D = x.shape
    F = w_down.shape[0]
    nj = F // tf
    n_slabs = 2 * (tf // FFN_SUB)
    assert tm % FFN_RB == 0 and tf % FFN_SUB == 0
    kernel = functools.partial(_ffn_kernel, tiles_per_seq=S // tm)
    return pl.pallas_call(
        kernel,
        out_shape=jax.ShapeDtypeStruct((T, D), F32),
        grid=(T // tm, nj),
        in_specs=[
            pl.BlockSpec((tm, D), lambda i, j: (i, 0)),
            pl.BlockSpec((1, D), lambda i, j: (0, 0)),
            pl.BlockSpec((D, tf), lambda i, j: (0, j)),
            pl.BlockSpec((D, tf), lambda i, j: (0, j + nj)),
            pl.BlockSpec((FFN_CONV, tf), lambda i, j: (0, j)),
            pl.BlockSpec((FFN_CONV, tf), lambda i, j: (0, j + nj)),
            pl.BlockSpec((1, tf), lambda i, j: (0, j)),
            pl.BlockSpec((1, tf), lambda i, j: (0, j + nj)),
            pl.BlockSpec((tf, D), lambda i, j: (j, 0)),
            pl.BlockSpec((1, D), lambda i, j: (0, 0)),
        ],
        out_specs=pl.BlockSpec((tm, D), lambda i, j: (i, 0)),
        scratch_shapes=[
            pltpu.VMEM((tm, D), BF16),
            pltpu.VMEM((tm, D), F32),
            pltpu.VMEM((nj, n_slabs, V7X_SUBLANES, FFN_SUB), F32),
        ] + [pltpu.VMEM((tm + V7X_SUBLANES, FFN_SUB), F32)] * n_slabs
          + [pltpu.VMEM((tm, FFN_SUB), BF16)] * (n_slabs // 2),
        compiler_params=_params("arbitrary", "arbitrary"),
        name="ffn",
    )(x, g3, w_up, w_up, conv_w, conv_w, conv_b, conv_b, w_down, gf)


def _layer(x, mem, posf, inv, B, S, norm1_g, w_in, ret_g, rg_conv_w, rg_conv_b, rg_wa, rg_ba, rg_wx, rg_bx,
           rg_lambda, w_out, norm2_g, norm_mem_g, xa_wq, xa_wk, xa_wv, xa_wo, norm3_g, ffn_w_up,
           ffn_conv_w, ffn_conv_b, ffn_w_down, out_g, tiles):
    W = LRU_BLOCKS * LRU_BLOCK_DIM
    row = lambda a: a.reshape(1, -1)
    h = _in_proj(x, row(norm1_g), posf, inv, w_in.astype(BF16), tiles["in_tm"], tiles["ret_chunk"])
    ret = _retention(h, row(ret_g), B, S, tiles["ret_chunk"])
    wg = jnp.concatenate([rg_wa, rg_wx], axis=-1).astype(BF16)
    lru = _rglru(h, rg_conv_w, row(rg_conv_b), wg, rg_ba.reshape(1, W), rg_bx.reshape(1, W),
                 row(rg_lambda), B, S, tiles["lru_tt"])
    kmem, vmem = _kv_proj(mem, row(norm_mem_g), xa_wk.astype(BF16), xa_wv.astype(BF16), tiles["kv_tn"])
    x2 = _mid(x, ret, lru, w_out.astype(BF16), row(norm2_g), xa_wq.astype(BF16), kmem, vmem,
              xa_wo.astype(BF16), B, S, tiles["mid_tm"])
    return _ffn(x2, row(norm3_g), ffn_w_up.astype(BF16), ffn_conv_w, row(ffn_conv_b),
                ffn_w_down.astype(BF16), out_g, S, tiles["ffn_tm"], tiles["ffn_tf"])


_TILES = dict(in_tm=IN_TM, ret_chunk=RET_CHUNK, lru_tt=LRU_TT, kv_tn=KV_TN, mid_tm=MID_TM,
              ffn_tm=FFN_TM, ffn_tf=FFN_TF)


def _block(x, mem, positions, norm1_g, w_in, ret_g, rg_conv_w, rg_conv_b, rg_wa, rg_ba, rg_wx, rg_bx,
           rg_lambda, w_out, norm2_g, norm_mem_g, xa_wq, xa_wk, xa_wv, xa_wo, norm3_g, ffn_w_up,
           ffn_conv_w, ffn_conv_b, ffn_w_down, final_g, tiles):
    B, S, D = x.shape
    assert w_in.shape[0] == 1, "only depth 1 is supported"
    T = B * S
    half = RET_HEAD_DIM // 2
    inv = (ROPE_BASE ** (-jnp.arange(half, dtype=F32) / half)).reshape(1, half)
    posf = jnp.broadcast_to(positions.reshape(T, 1).astype(F32), (T, half))
    out = _layer(x.reshape(T, D), mem.reshape(B * mem.shape[1], D), posf, inv, B, S, norm1_g[0], w_in[0],
                 ret_g[0], rg_conv_w[0], rg_conv_b[0], rg_wa[0], rg_ba[0], rg_wx[0], rg_bx[0], rg_lambda[0],
                 w_out[0], norm2_g[0], norm_mem_g[0], xa_wq[0], xa_wk[0], xa_wv[0], xa_wo[0], norm3_g[0],
                 ffn_w_up[0], ffn_conv_w[0], ffn_conv_b[0], ffn_w_down[0], final_g.reshape(1, D), tiles)
    return out.reshape(B, S, D)


def kernel(x, mem, positions, norm1_g, w_in, ret_g, rg_conv_w, rg_conv_b, rg_wa, rg_ba, rg_wx, rg_bx, rg_lambda, w_out, norm2_g, norm_mem_g, xa_wq, xa_wk, xa_wv, xa_wo, norm3_g, ffn_w_up, ffn_conv_w, ffn_conv_b, ffn_w_down, final_g):
    return _block(x, mem, positions, norm1_g, w_in, ret_g, rg_conv_w, rg_conv_b, rg_wa, rg_ba, rg_wx, rg_bx,
                  rg_lambda, w_out, norm2_g, norm_mem_g, xa_wq, xa_wk, xa_wv, xa_wo, norm3_g, ffn_w_up,
                  ffn_conv_w, ffn_conv_b, ffn_w_down, final_g, _TILES)
```

```python
import functools
import math

import jax
import jax.numpy as jnp
from jax import lax
from jax.experimental import pallas as pl
from jax.experimental.pallas import tpu as pltpu

F32 = jnp.float32
BF16 = jnp.bfloat16

EPS = 1e-6
ROPE_BASE = 10000.0
RET_HEADS = 4
RET_HEAD_DIM = 256
LRU_BLOCKS = 8
LRU_BLOCK_DIM = 128
LRU_CONV = 4
LRU_C = 8.0
XA_HEADS = 4
FFN_CONV = 3

V7X_LANES = 128
V7X_SUBLANES = 8
V7X_F32_TINY = 1.1754944e-38
V7X_VMEM_LIMIT = 56 * 1024 * 1024

IN_TM = 512
RET_CHUNK = 256
LRU_TT = 256
LRU_UNROLL = 8
MID_TM = 256
FFN_TM = 512
FFN_TF = 512
FFN_SUB = 256
KV_TN = 512


def _params(*sem):
    return pltpu.CompilerParams(dimension_semantics=sem, vmem_limit_bytes=V7X_VMEM_LIMIT)


def _rms(x, g):
    return x * lax.rsqrt(jnp.mean(x * x, axis=-1, keepdims=True) + EPS) * g


def _sigmoid(x):
    return 1.0 / (1.0 + jnp.exp(-x))


def _resident(shape):
    nd = len(shape)
    return pl.BlockSpec(shape, lambda *_: (0,) * nd, pipeline_mode=pl.Buffered(1))


def _in_proj_kernel(x_ref, g_ref, pos_ref, inv_ref, dec_ref, w_ref, o_ref, xn_ref, *, chunk):
    tm = x_ref.shape[0]
    tn = RET_HEADS * RET_HEAD_DIM
    half = RET_HEAD_DIM // 2
    xn_ref[...] = _rms(x_ref[...], g_ref[...]).astype(BF16)
    xn = xn_ref[...]
    n_groups = o_ref.shape[1] // tn
    cos = sin = None
    for grp in list(range(2, n_groups)) + [0, 1]:
        cs = slice(grp * tn, (grp + 1) * tn)
        h = jnp.dot(xn, w_ref[:, cs], preferred_element_type=F32)
        if cos is None:
            ang = pos_ref[...] * inv_ref[...]
            cos = jnp.cos(ang)
            sin = jnp.sin(ang)
        if grp == 3:
            h = h * _sigmoid(h)
        if grp == 5:
            h = 0.5 * h * (1.0 + jnp.tanh(math.sqrt(2.0 / math.pi) * (h + 0.044715 * (h * h * h))))
        if grp >= 2:
            o_ref[:, cs] = h.astype(BF16)
            continue
        for r0 in range(0, tm, chunk):
            rs = slice(r0, r0 + chunk)
            for hd in range(RET_HEADS):
                d = dec_ref[grp, :, hd * half:(hd + 1) * half]
                c = cos[rs] * d
                s = sin[rs] * d
                lo = hd * RET_HEAD_DIM
                t1 = h[rs, lo:lo + half]
                t2 = h[rs, lo + half:lo + RET_HEAD_DIM]
                o_ref[rs, grp * tn + lo:grp * tn + lo + half] = (t1 * c - t2 * s).astype(BF16)
                o_ref[rs, grp * tn + lo + half:grp * tn + lo + RET_HEAD_DIM] = (t1 * s + t2 * c).astype(BF16)


def _retention_log_gamma():
    return jnp.log(1.0 - 2.0 ** (-5.0 - jnp.arange(RET_HEADS, dtype=F32)))


def _in_proj(x, g, posf, inv, w, tm, C):
    T, D = x.shape
    N = w.shape[1]
    half = RET_HEAD_DIM // 2
    assert tm % C == 0
    t = jnp.arange(C, dtype=F32)
    log_g = _retention_log_gamma()
    q_dec = jnp.exp(log_g[None, :] * (t[:, None] + 1.0))
    k_dec = jnp.exp(log_g[None, :] * (C - 1.0 - t[:, None])) * (RET_HEAD_DIM ** -0.5)
    dec = jnp.repeat(jnp.stack([q_dec, k_dec]), half, axis=-1)
    return pl.pallas_call(
        functools.partial(_in_proj_kernel, chunk=C),
        out_shape=jax.ShapeDtypeStruct((T, N), BF16),
        grid=(T // tm,),
        in_specs=[
            pl.BlockSpec((tm, D), lambda i: (i, 0)),
            pl.BlockSpec((1, D), lambda i: (0, 0)),
            pl.BlockSpec((tm, half), lambda i: (i, 0)),
            pl.BlockSpec((1, half), lambda i: (0, 0)),
            _resident((2, C, RET_HEADS * half)),
            _resident((D, N)),
        ],
        out_specs=pl.BlockSpec((tm, N), lambda i: (i, 0)),
        scratch_shapes=[pltpu.VMEM((tm, D), BF16)],
        compiler_params=_params("arbitrary"),
        name="in_proj",
    )(x, g, posf, inv, dec, w)


def _retention_kernel(q_ref, k_ref, v_ref, g_ref, mask_ref, cdec_ref, rg_ref, o_ref, state_ref):
    n = pl.program_id(1)

    @pl.when(n == 0)
    def _():
        state_ref[...] = jnp.zeros_like(state_ref)

    dh = RET_HEAD_DIM
    for hd in range(RET_HEADS):
        sl = slice(hd * dh, (hd + 1) * dh)
        q = q_ref[:, sl]
        k = k_ref[:, sl]
        v = v_ref[:, sl]
        state = state_ref[hd]
        s = lax.dot_general(q, k, (((1,), (1,)), ((), ())), preferred_element_type=F32) * mask_ref[hd]
        ret = (jnp.dot(s.astype(BF16), v, preferred_element_type=F32)
               + jnp.dot(q, state.astype(BF16), preferred_element_type=F32))
        state_ref[hd] = state * cdec_ref[hd] + lax.dot_general(
            k, v, (((0,), (0,)), ((), ())), preferred_element_type=F32)
        ret = ret * lax.rsqrt(jnp.mean(ret * ret, axis=-1, keepdims=True) + EPS)
        o_ref[:, sl] = (ret * rg_ref[:, sl] * g_ref[:, sl].astype(F32)).astype(BF16)


def _retention(h, ret_g, B, S, C):
    T = h.shape[0]
    R = RET_HEADS * RET_HEAD_DIM
    N = S // C
    H, dh = RET_HEADS, RET_HEAD_DIM
    log_g = _retention_log_gamma()
    idx = jnp.arange(C)
    mask = jnp.where(idx[:, None] >= idx[None, :], jnp.exp(-log_g * C)[:, None, None], 0.0)
    c_dec = jnp.broadcast_to(jnp.exp(log_g * C)[:, None, None], (H, 1, dh))

    def col(c):
        return pl.BlockSpec((C, R), lambda b, n: (b * N + n, c))

    return pl.pallas_call(
        _retention_kernel,
        out_shape=jax.ShapeDtypeStruct((T, R), BF16),
        grid=(B, N),
        in_specs=[
            col(0), col(1), col(2), col(3),
            pl.BlockSpec((H, C, C), lambda b, n: (0, 0, 0)),
            pl.BlockSpec((H, 1, dh), lambda b, n: (0, 0, 0)),
            pl.BlockSpec((1, R), lambda b, n: (0, 0)),
        ],
        out_specs=pl.BlockSpec((C, R), lambda b, n: (b * N + n, 0)),
        scratch_shapes=[pltpu.VMEM((H, dh, dh), F32)],
        compiler_params=_params("arbitrary", "arbitrary"),
        name="retention",
    )(h, h, h, h, mask, c_dec, ret_g)


def _rglru_kernel(u_ref, y_ref, cw_ref, cb_ref, wg_ref, ba_ref, bx_ref, lam_ref,
                  o_ref, ubuf_ref, utail_ref, a_ref, b_ref, hcar_ref):
    n = pl.program_id(0)
    B, tt, W = u_ref.shape
    halo = V7X_SUBLANES
    bd = LRU_BLOCK_DIM
    pitch = a_ref.shape[1] // B

    @pl.when(n == 0)
    def _():
        utail_ref[...] = jnp.zeros_like(utail_ref)
        hcar_ref[...] = jnp.zeros_like(hcar_ref)

    lam = lam_ref[...]
    log_sig = jnp.minimum(lam, 0.0) - jnp.log1p(jnp.exp(-jnp.abs(lam)))
    for s in range(B):
        ubuf_ref[0:halo, :] = utail_ref[s]
        ubuf_ref[halo:halo + tt, :] = u_ref[s].astype(F32)
        utail_ref[s] = ubuf_ref[tt:tt + halo, :]
        uc = cb_ref[...]
        for kk in range(LRU_CONV):
            off = halo - (LRU_CONV - 1) + kk
            uc = uc + cw_ref[kk:kk + 1, :] * ubuf_ref[off:off + tt, :]
        for nb in range(LRU_BLOCKS):
            sl = slice(nb * bd, (nb + 1) * bd)
            ub = uc[:, sl]
            z = jnp.dot(ub.astype(BF16), wg_ref[nb], preferred_element_type=F32)
            r = _sigmoid(z[:, :bd] + ba_ref[:, sl])
            ig = _sigmoid(z[:, bd:] + bx_ref[:, sl])
            log_a = LRU_C * r * log_sig[:, sl]
            a_ref[nb, s * pitch:s * pitch + tt, :] = jnp.exp(log_a)
            th = jnp.tanh(log_a)
            q = -2.0 * th / (1.0 - th)
            b_ref[nb, s * pitch:s * pitch + tt, :] = q * lax.rsqrt(jnp.maximum(q, V7X_F32_TINY)) * (ig * ub)

    def step(t, hs):
        out = []
        for nb in range(LRU_BLOCKS):
            rows = pl.ds(t, B, stride=pitch)
            hn = a_ref[nb, rows, :] * hs[nb] + b_ref[nb, rows, :]
            b_ref[nb, rows, :] = hn
            out.append(hn)
        return tuple(out)

    h0 = tuple(hcar_ref[nb] for nb in range(LRU_BLOCKS))
    hT = lax.fori_loop(0, tt, step, h0, unroll=LRU_UNROLL)
    for nb in range(LRU_BLOCKS):
        hcar_ref[nb] = hT[nb]

    for s in range(B):
        for nb in range(LRU_BLOCKS):
            sl = slice(nb * bd, (nb + 1) * bd)
            o_ref[s, :, sl] = (b_ref[nb, s * pitch:s * pitch + tt, :] * y_ref[s, :, sl].astype(F32)).astype(BF16)


def _rglru(h, conv_w, conv_b, wg, ba, bx, lam, B, S, tt):
    T = h.shape[0]
    W = LRU_BLOCKS * LRU_BLOCK_DIM
    h3 = h.reshape(B, S, h.shape[1])
    pitch = tt + V7X_SUBLANES
    assert B <= 4 and tt % (4 * V7X_SUBLANES) == 0

    def const(shape):
        nd = len(shape)
        return pl.BlockSpec(shape, lambda n: (0,) * nd)

    out = pl.pallas_call(
        _rglru_kernel,
        out_shape=jax.ShapeDtypeStruct((B, S, W), BF16),
        grid=(S // tt,),
        in_specs=[
            pl.BlockSpec((B, tt, W), lambda n: (0, n, 4)),
            pl.BlockSpec((B, tt, W), lambda n: (0, n, 5)),
            const((LRU_CONV, W)), const((1, W)),
            const((LRU_BLOCKS, LRU_BLOCK_DIM, 2 * LRU_BLOCK_DIM)),
            const((1, W)), const((1, W)), const((1, W)),
        ],
        out_specs=pl.BlockSpec((B, tt, W), lambda n: (0, n, 0)),
        scratch_shapes=[
            pltpu.VMEM((tt + V7X_SUBLANES, W), F32),
            pltpu.VMEM((B, V7X_SUBLANES, W), F32),
            pltpu.VMEM((LRU_BLOCKS, B * pitch, LRU_BLOCK_DIM), F32),
            pltpu.VMEM((LRU_BLOCKS, B * pitch, LRU_BLOCK_DIM), F32),
            pltpu.VMEM((LRU_BLOCKS, B, LRU_BLOCK_DIM), F32),
        ],
        compiler_params=_params("arbitrary"),
        name="rglru",
    )(h3, h3, conv_w, conv_b, wg, ba, bx, lam)
    return out.reshape(T, W)


def _kv_proj_kernel(m_ref, g_ref, wk_ref, wv_ref, k_ref, v_ref, mn_ref):
    @pl.when(pl.program_id(0) == 0)
    def _():
        mn_ref[...] = _rms(m_ref[...], g_ref[...]).astype(BF16)

    mn = mn_ref[...]
    k_ref[...] = jnp.dot(mn, wk_ref[...], preferred_element_type=F32).astype(BF16)
    v_ref[...] = jnp.dot(mn, wv_ref[...], preferred_element_type=F32).astype(BF16)


def _kv_proj(mem, g, wk, wv, tn):
    M, D = mem.shape
    N = wk.shape[1]
    return pl.pallas_call(
        _kv_proj_kernel,
        out_shape=(jax.ShapeDtypeStruct((M, N), BF16), jax.ShapeDtypeStruct((M, N), BF16)),
        grid=(N // tn,),
        in_specs=[
            pl.BlockSpec((M, D), lambda j: (0, 0)),
            pl.BlockSpec((1, D), lambda j: (0, 0)),
            pl.BlockSpec((D, tn), lambda j: (0, j)),
            pl.BlockSpec((D, tn), lambda j: (0, j)),
        ],
        out_specs=(pl.BlockSpec((M, tn), lambda j: (0, j)), pl.BlockSpec((M, tn), lambda j: (0, j))),
        scratch_shapes=[pltpu.VMEM((M, D), BF16)],
        compiler_params=_params("arbitrary"),
        name="kv_proj",
    )(mem, g, wk, wv)


def _mid_kernel(x_ref, ret_ref, lru_ref, wout_ref, g2_ref, wq_ref, k_ref, v_ref, wo_ref, o_ref, att_ref):
    R = ret_ref.shape[1]
    x1 = (x_ref[...]
          + jnp.dot(ret_ref[...], wout_ref[0:R, :], preferred_element_type=F32)
          + jnp.dot(lru_ref[...], wout_ref[R:, :], preferred_element_type=F32))
    xn = _rms(x1, g2_ref[...]).astype(BF16)
    q = jnp.dot(xn, wq_ref[...], preferred_element_type=F32).astype(BF16)
    D = q.shape[1]
    dh = D // XA_HEADS
    scale = dh ** -0.5
    for hd in range(XA_HEADS):
        sl = slice(hd * dh, (hd + 1) * dh)
        s = lax.dot_general(q[:, sl], k_ref[:, sl], (((1,), (1,)), ((), ())),
                            preferred_element_type=F32) * scale
        e = jnp.exp(s - jnp.max(s, axis=-1, keepdims=True))
        p = e / jnp.sum(e, axis=-1, keepdims=True)
        att_ref[:, sl] = jnp.dot(p.astype(BF16), v_ref[:, sl], preferred_element_type=F32).astype(BF16)
    o_ref[...] = x1 + jnp.dot(att_ref[...], wo_ref[...], preferred_element_type=F32)


def _mid(x, ret, lru, w_out, g2, wq, kmem, vmem, wo, B, S, tm):
    T, D = x.shape
    R = ret.shape[1]
    M = kmem.shape[0] // B
    per_b = S // tm
    return pl.pallas_call(
        _mid_kernel,
        out_shape=jax.ShapeDtypeStruct((T, D), F32),
        grid=(T // tm,),
        in_specs=[
            pl.BlockSpec((tm, D), lambda i: (i, 0)),
            pl.BlockSpec((tm, R), lambda i: (i, 0)),
            pl.BlockSpec((tm, R), lambda i: (i, 0)),
            _resident((D, D)),
            pl.BlockSpec((1, D), lambda i: (0, 0)),
            _resident((D, D)),
            pl.BlockSpec((M, D), lambda i: (i // per_b, 0)),
            pl.BlockSpec((M, D), lambda i: (i // per_b, 0)),
            _resident((D, D)),
        ],
        out_specs=pl.BlockSpec((tm, D), lambda i: (i, 0)),
        scratch_shapes=[pltpu.VMEM((tm, D), BF16)],
        compiler_params=_params("arbitrary"),
        name="mid",
    )(x, ret, lru, w_out, g2, wq, kmem, vmem, wo)


def _ffn_kernel(x_ref, g3_ref, wa_ref, wb_ref, cwa_ref, cwb_ref, cba_ref, cbb_ref, wd_ref, gf_ref,
                o_ref, xn_ref, acc_ref, tail_ref, *hbuf_refs, tiles_per_seq):
    i = pl.program_id(0)
    j = pl.program_id(1)
    nj = pl.num_programs(1)
    tm = x_ref.shape[0]
    halo = V7X_SUBLANES
    n_sub = wd_ref.shape[0] // FFN_SUB

    @pl.when(j == 0)
    def _():
        xn_ref[...] = _rms(x_ref[...], g3_ref[...]).astype(BF16)
        acc_ref[...] = jnp.zeros_like(acc_ref)

    @pl.when(jnp.logical_and(j == 0, (i % tiles_per_seq) == 0))
    def _():
        tail_ref[...] = jnp.zeros_like(tail_ref)

    xn = xn_ref[...]

    def up(c, half, w_ref):
        slab = half * n_sub + c
        h = jnp.dot(xn, w_ref[:, c * FFN_SUB:(c + 1) * FFN_SUB], preferred_element_type=F32)
        hbuf_refs[slab][halo:halo + tm, :] = h
        hbuf_refs[slab][0:halo, :] = tail_ref[j, slab]
        tail_ref[j, slab] = h[tm - halo:tm, :]

    def conv(c, half, cw_ref, cb_ref):
        cs = slice(c * FFN_SUB, (c + 1) * FFN_SUB)
        hbuf_ref = hbuf_refs[half * n_sub + c]
        y = cb_ref[:, cs]
        for kk in range(FFN_CONV):
            off = halo - (FFN_CONV - 1) + kk
            y = y + cw_ref[kk:kk + 1, cs] * hbuf_ref[off:off + tm, :]
        return y

    for c in range(n_sub):
        up(c, 0, wa_ref)
        up(c, 1, wb_ref)
    part = None
    for c in range(n_sub):
        ya = conv(c, 0, cwa_ref, cba_ref)
        yb = conv(c, 1, cwb_ref, cbb_ref)
        gated = (ya * _sigmoid(ya) * yb).astype(BF16)
        d = jnp.dot(gated, wd_ref[c * FFN_SUB:(c + 1) * FFN_SUB, :], preferred_element_type=F32)
        part = d if part is None else part + d
    acc_ref[...] += part

    @pl.when(j == nj - 1)
    def _():
        o_ref[...] = _rms(x_ref[...] + acc_ref[...], gf_ref[...])


def _ffn(x, g3, w_up, conv_w, conv_b, w_down, gf, S, tm, tf):
    T, D = x.shape
    F = w_down.shape[0]
    nj = F // tf
    n_slabs = 2 * (tf // FFN_SUB)
    kernel = functools.partial(_ffn_kernel, tiles_per_seq=S // tm)
    return pl.pallas_call(
        kernel,
        out_shape=jax.ShapeDtypeStruct((T, D), F32),
        grid=(T // tm, nj),
        in_specs=[
            pl.BlockSpec((tm, D), lambda i, j: (i, 0)),
            pl.BlockSpec((1, D), lambda i, j: (0, 0)),
            pl.BlockSpec((D, tf), lambda i, j: (0, j)),
            pl.BlockSpec((D, tf), lambda i, j: (0, j + nj)),
            pl.BlockSpec((FFN_CONV, tf), lambda i, j: (0, j)),
            pl.BlockSpec((FFN_CONV, tf), lambda i, j: (0, j + nj)),
            pl.BlockSpec((1, tf), lambda i, j: (0, j)),
            pl.BlockSpec((1, tf), lambda i, j: (0, j + nj)),
            pl.BlockSpec((tf, D), lambda i, j: (j, 0)),
            pl.BlockSpec((1, D), lambda i, j: (0, 0)),
        ],
        out_specs=pl.BlockSpec((tm, D), lambda i, j: (i, 0)),
        scratch_shapes=[
            pltpu.VMEM((tm, D), BF16),
            pltpu.VMEM((tm, D), F32),
            pltpu.VMEM((nj, n_slabs, V7X_SUBLANES, FFN_SUB), F32),
        ] + [pltpu.VMEM((tm + V7X_SUBLANES, FFN_SUB), F32)] * n_slabs,
        compiler_params=_params("arbitrary", "arbitrary"),
        name="ffn",
    )(x, g3, w_up, w_up, conv_w, conv_w, conv_b, conv_b, w_down, gf)


def _layer(x, mem, posf, inv, B, S, norm1_g, w_in, ret_g, rg_conv_w, rg_conv_b, rg_wa, rg_ba, rg_wx, rg_bx,
           rg_lambda, w_out, norm2_g, norm_mem_g, xa_wq, xa_wk, xa_wv, xa_wo, norm3_g, ffn_w_up,
           ffn_conv_w, ffn_conv_b, ffn_w_down, out_g, tiles):
    W = LRU_BLOCKS * LRU_BLOCK_DIM
    row = lambda a: a.reshape(1, -1)
    h = _in_proj(x, row(norm1_g), posf, inv, w_in.astype(BF16), tiles["in_tm"], tiles["ret_chunk"])
    ret = _retention(h, row(ret_g), B, S, tiles["ret_chunk"])
    wg = jnp.concatenate([rg_wa, rg_wx], axis=-1).astype(BF16)
    lru = _rglru(h, rg_conv_w, row(rg_conv_b), wg, rg_ba.reshape(1, W), rg_bx.reshape(1, W),
                 row(rg_lambda), B, S, tiles["lru_tt"])
    kmem, vmem = _kv_proj(mem, row(norm_mem_g), xa_wk.astype(BF16), xa_wv.astype(BF16), tiles["kv_tn"])
    x2 = _mid(x, ret, lru, w_out.astype(BF16), row(norm2_g), xa_wq.astype(BF16), kmem, vmem,
              xa_wo.astype(BF16), B, S, tiles["mid_tm"])
    return _ffn(x2, row(norm3_g), ffn_w_up.astype(BF16), ffn_conv_w, row(ffn_conv_b),
                ffn_w_down.astype(BF16), out_g, S, tiles["ffn_tm"], tiles["ffn_tf"])


_TILES = dict(in_tm=IN_TM, ret_chunk=RET_CHUNK, lru_tt=LRU_TT, kv_tn=KV_TN, mid_tm=MID_TM,
              ffn_tm=FFN_TM, ffn_tf=FFN_TF)


def _block(x, mem, positions, norm1_g, w_in, ret_g, rg_conv_w, rg_conv_b, rg_wa, rg_ba, rg_wx, rg_bx,
           rg_lambda, w_out, norm2_g, norm_mem_g, xa_wq, xa_wk, xa_wv, xa_wo, norm3_g, ffn_w_up,
           ffn_conv_w, ffn_conv_b, ffn_w_down, final_g, tiles):
    B, S, D = x.shape
    assert w_in.shape[0] == 1, "only depth 1 is supported"
    T = B * S
    half = RET_HEAD_DIM // 2
    inv = (ROPE_BASE ** (-jnp.arange(half, dtype=F32) / half)).reshape(1, half)
    posf = jnp.broadcast_to(positions.reshape(T, 1).astype(F32), (T, half))
    out = _layer(x.reshape(T, D), mem.reshape(B * mem.shape[1], D), posf, inv, B, S, norm1_g[0], w_in[0],
                 ret_g[0], rg_conv_w[0], rg_conv_b[0], rg_wa[0], rg_ba[0], rg_wx[0], rg_bx[0], rg_lambda[0],
                 w_out[0], norm2_g[0], norm_mem_g[0], xa_wq[0], xa_wk[0], xa_wv[0], xa_wo[0], norm3_g[0],
                 ffn_w_up[0], ffn_conv_w[0], ffn_conv_b[0], ffn_w_down[0], final_g.reshape(1, D), tiles)
    return out.reshape(B, S, D)


def kernel(x, mem, positions, norm1_g, w_in, ret_g, rg_conv_w, rg_conv_b, rg_wa, rg_ba, rg_wx, rg_bx, rg_lambda, w_out, norm2_g, norm_mem_g, xa_wq, xa_wk, xa_wv, xa_wo, norm3_g, ffn_w_up, ffn_conv_w, ffn_conv_b, ffn_w_down, final_g):
    return _block(x, mem, positions, norm1_g, w_in, ret_g, rg_conv_w, rg_conv_b, rg_wa, rg_ba, rg_wx, rg_bx,
                  rg_lambda, w_out, norm2_g, norm_mem_g, xa_wq, xa_wk, xa_wv, xa_wo, norm3_g, ffn_w_up,
                  ffn_conv_w, ffn_conv_b, ffn_w_down, final_g, _TILES)
```

```python
import functools
import math

import jax
import jax.numpy as jnp
from jax import lax
from jax.experimental import pallas as pl
from jax.experimental.pallas import tpu as pltpu

F32 = jnp.float32
BF16 = jnp.bfloat16

EPS = 1e-6
ROPE_BASE = 10000.0
RET_HEADS = 4
RET_HEAD_DIM = 256
LRU_BLOCKS = 8
LRU_BLOCK_DIM = 128
LRU_CONV = 4
LRU_C = 8.0
XA_HEADS = 4
FFN_CONV = 3

V7X_LANES = 128
V7X_SUBLANES = 8
V7X_F32_TINY = 1.1754944e-38
V7X_VMEM_LIMIT = 56 * 1024 * 1024

IN_TM = 512
RET_CHUNK = 256
RET_CHUNKS_PER_STEP = 4
LRU_TT = 256
LRU_UNROLL = 8
MID_TM = 256
FFN_TM = 512
FFN_TF = 512
FFN_SUB = 256
KV_TN = 512


def _params(*sem):
    return pltpu.CompilerParams(dimension_semantics=sem, vmem_limit_bytes=V7X_VMEM_LIMIT)


def _rms(x, g):
    return x * lax.rsqrt(jnp.mean(x * x, axis=-1, keepdims=True) + EPS) * g


def _sigmoid(x):
    return 1.0 / (1.0 + jnp.exp(-x))


def _resident(shape):
    nd = len(shape)
    return pl.BlockSpec(shape, lambda *_: (0,) * nd, pipeline_mode=pl.Buffered(1))


def _in_proj_kernel(x_ref, g_ref, pos_ref, inv_ref, dec_ref, w_ref, o_ref, xn_ref, *, chunk):
    tm = x_ref.shape[0]
    tn = RET_HEADS * RET_HEAD_DIM
    half = RET_HEAD_DIM // 2
    xn_ref[...] = _rms(x_ref[...], g_ref[...]).astype(BF16)
    xn = xn_ref[...]
    n_groups = o_ref.shape[1] // tn
    cos = sin = None
    for grp in list(range(2, n_groups)) + [0, 1]:
        cs = slice(grp * tn, (grp + 1) * tn)
        h = jnp.dot(xn, w_ref[:, cs], preferred_element_type=F32)
        if cos is None:
            ang = pos_ref[...] * inv_ref[...]
            cos = jnp.cos(ang)
            sin = jnp.sin(ang)
        if grp == 3:
            h = h * _sigmoid(h)
        if grp == 5:
            h = 0.5 * h * (1.0 + jnp.tanh(math.sqrt(2.0 / math.pi) * (h + 0.044715 * (h * h * h))))
        if grp >= 2:
            o_ref[:, cs] = h.astype(BF16)
            continue
        for r0 in range(0, tm, chunk):
            rs = slice(r0, r0 + chunk)
            for hd in range(RET_HEADS):
                d = dec_ref[grp, :, hd * half:(hd + 1) * half]
                c = cos[rs] * d
                s = sin[rs] * d
                lo = hd * RET_HEAD_DIM
                t1 = h[rs, lo:lo + half]
                t2 = h[rs, lo + half:lo + RET_HEAD_DIM]
                o_ref[rs, grp * tn + lo:grp * tn + lo + half] = (t1 * c - t2 * s).astype(BF16)
                o_ref[rs, grp * tn + lo + half:grp * tn + lo + RET_HEAD_DIM] = (t1 * s + t2 * c).astype(BF16)


def _retention_log_gamma():
    return jnp.log(1.0 - 2.0 ** (-5.0 - jnp.arange(RET_HEADS, dtype=F32)))


def _in_proj(x, g, posf, inv, w, tm, C):
    T, D = x.shape
    N = w.shape[1]
    half = RET_HEAD_DIM // 2
    assert tm % C == 0
    t = jnp.arange(C, dtype=F32)
    log_g = _retention_log_gamma()
    q_dec = jnp.exp(log_g[None, :] * (t[:, None] + 1.0))
    k_dec = jnp.exp(log_g[None, :] * (C - 1.0 - t[:, None])) * (RET_HEAD_DIM ** -0.5)
    dec = jnp.repeat(jnp.stack([q_dec, k_dec]), half, axis=-1)
    return pl.pallas_call(
        functools.partial(_in_proj_kernel, chunk=C),
        out_shape=jax.ShapeDtypeStruct((T, N), BF16),
        grid=(T // tm,),
        in_specs=[
            pl.BlockSpec((tm, D), lambda i: (i, 0)),
            pl.BlockSpec((1, D), lambda i: (0, 0)),
            pl.BlockSpec((tm, half), lambda i: (i, 0)),
            pl.BlockSpec((1, half), lambda i: (0, 0)),
            _resident((2, C, RET_HEADS * half)),
            _resident((D, N)),
        ],
        out_specs=pl.BlockSpec((tm, N), lambda i: (i, 0)),
        scratch_shapes=[pltpu.VMEM((tm, D), BF16)],
        compiler_params=_params("arbitrary"),
        name="in_proj",
    )(x, g, posf, inv, dec, w)


def _retention_kernel(q_ref, k_ref, v_ref, g_ref, mask_ref, cdec_ref, rg_ref, o_ref, state_ref):
    n = pl.program_id(1)

    @pl.when(n == 0)
    def _():
        state_ref[...] = jnp.zeros_like(state_ref)

    dh = RET_HEAD_DIM
    C = mask_ref.shape[1]
    for r0 in range(0, q_ref.shape[0], C):
        rs = slice(r0, r0 + C)
        for hd in range(RET_HEADS):
            sl = slice(hd * dh, (hd + 1) * dh)
            q = q_ref[rs, sl]
            k = k_ref[rs, sl]
            v = v_ref[rs, sl]
            state = state_ref[hd]
            s = lax.dot_general(q, k, (((1,), (1,)), ((), ())), preferred_element_type=F32) * mask_ref[hd]
            ret = (jnp.dot(s.astype(BF16), v, preferred_element_type=F32)
                   + jnp.dot(q, state.astype(BF16), preferred_element_type=F32))
            state_ref[hd] = state * cdec_ref[hd] + lax.dot_general(
                k, v, (((0,), (0,)), ((), ())), preferred_element_type=F32)
            ret = ret * lax.rsqrt(jnp.mean(ret * ret, axis=-1, keepdims=True) + EPS)
            o_ref[rs, sl] = (ret * rg_ref[:, sl] * g_ref[rs, sl].astype(F32)).astype(BF16)


def _retention(h, ret_g, B, S, C):
    T = h.shape[0]
    R = RET_HEADS * RET_HEAD_DIM
    rows = C * RET_CHUNKS_PER_STEP
    N = S // rows
    H, dh = RET_HEADS, RET_HEAD_DIM
    log_g = _retention_log_gamma()
    idx = jnp.arange(C)
    mask = jnp.where(idx[:, None] >= idx[None, :], jnp.exp(-log_g * C)[:, None, None], 0.0)
    c_dec = jnp.broadcast_to(jnp.exp(log_g * C)[:, None, None], (H, 1, dh))

    def col(c):
        return pl.BlockSpec((rows, R), lambda b, n: (b * N + n, c))

    return pl.pallas_call(
        _retention_kernel,
        out_shape=jax.ShapeDtypeStruct((T, R), BF16),
        grid=(B, N),
        in_specs=[
            col(0), col(1), col(2), col(3),
            pl.BlockSpec((H, C, C), lambda b, n: (0, 0, 0)),
            pl.BlockSpec((H, 1, dh), lambda b, n: (0, 0, 0)),
            pl.BlockSpec((1, R), lambda b, n: (0, 0)),
        ],
        out_specs=pl.BlockSpec((rows, R), lambda b, n: (b * N + n, 0)),
        scratch_shapes=[pltpu.VMEM((H, dh, dh), F32)],
        compiler_params=_params("arbitrary", "arbitrary"),
        name="retention",
    )(h, h, h, h, mask, c_dec, ret_g)


def _rglru_kernel(u_ref, y_ref, cw_ref, cb_ref, wg_ref, ba_ref, bx_ref, lam_ref,
                  o_ref, ubuf_ref, utail_ref, a_ref, b_ref, hcar_ref):
    n = pl.program_id(0)
    B, tt, W = u_ref.shape
    halo = V7X_SUBLANES
    bd = LRU_BLOCK_DIM
    pitch = a_ref.shape[1] // B

    @pl.when(n == 0)
    def _():
        utail_ref[...] = jnp.zeros_like(utail_ref)
        hcar_ref[...] = jnp.zeros_like(hcar_ref)

    lam = lam_ref[...]
    log_sig = jnp.minimum(lam, 0.0) - jnp.log1p(jnp.exp(-jnp.abs(lam)))
    for s in range(B):
        ubuf_ref[0:halo, :] = utail_ref[s]
        ubuf_ref[halo:halo + tt, :] = u_ref[s].astype(F32)
        utail_ref[s] = ubuf_ref[tt:tt + halo, :]
        uc = cb_ref[...]
        for kk in range(LRU_CONV):
            off = halo - (LRU_CONV - 1) + kk
            uc = uc + cw_ref[kk:kk + 1, :] * ubuf_ref[off:off + tt, :]
        for nb in range(LRU_BLOCKS):
            sl = slice(nb * bd, (nb + 1) * bd)
            ub = uc[:, sl]
            z = jnp.dot(ub.astype(BF16), wg_ref[nb], preferred_element_type=F32)
            r = _sigmoid(z[:, :bd] + ba_ref[:, sl])
            ig = _sigmoid(z[:, bd:] + bx_ref[:, sl])
            log_a = LRU_C * r * log_sig[:, sl]
            a_ref[nb, s * pitch:s * pitch + tt, :] = jnp.exp(log_a)
            th = jnp.tanh(log_a)
            q = -2.0 * th / (1.0 - th)
            b_ref[nb, s * pitch:s * pitch + tt, :] = q * lax.rsqrt(jnp.maximum(q, V7X_F32_TINY)) * (ig * ub)

    def step(t, hs):
        out = []
        for nb in range(LRU_BLOCKS):
            rows = pl.ds(t, B, stride=pitch)
            hn = a_ref[nb, rows, :] * hs[nb] + b_ref[nb, rows, :]
            b_ref[nb, rows, :] = hn
            out.append(hn)
        return tuple(out)

    h0 = tuple(hcar_ref[nb] for nb in range(LRU_BLOCKS))
    hT = lax.fori_loop(0, tt, step, h0, unroll=LRU_UNROLL)
    for nb in range(LRU_BLOCKS):
        hcar_ref[nb] = hT[nb]

    for s in range(B):
        for nb in range(LRU_BLOCKS):
            sl = slice(nb * bd, (nb + 1) * bd)
            o_ref[s, :, sl] = (b_ref[nb, s * pitch:s * pitch + tt, :] * y_ref[s, :, sl].astype(F32)).astype(BF16)


def _rglru(h, conv_w, conv_b, wg, ba, bx, lam, B, S, tt):
    T = h.shape[0]
    W = LRU_BLOCKS * LRU_BLOCK_DIM
    h3 = h.reshape(B, S, h.shape[1])
    pitch = tt + V7X_SUBLANES
    assert B <= 4 and tt % (4 * V7X_SUBLANES) == 0

    def const(shape):
        nd = len(shape)
        return pl.BlockSpec(shape, lambda n: (0,) * nd)

    out = pl.pallas_call(
        _rglru_kernel,
        out_shape=jax.ShapeDtypeStruct((B, S, W), BF16),
        grid=(S // tt,),
        in_specs=[
            pl.BlockSpec((B, tt, W), lambda n: (0, n, 4)),
            pl.BlockSpec((B, tt, W), lambda n: (0, n, 5)),
            const((LRU_CONV, W)), const((1, W)),
            const((LRU_BLOCKS, LRU_BLOCK_DIM, 2 * LRU_BLOCK_DIM)),
            const((1, W)), const((1, W)), const((1, W)),
        ],
        out_specs=pl.BlockSpec((B, tt, W), lambda n: (0, n, 0)),
        scratch_shapes=[
            pltpu.VMEM((tt + V7X_SUBLANES, W), F32),
            pltpu.VMEM((B, V7X_SUBLANES, W), F32),
            pltpu.VMEM((LRU_BLOCKS, B * pitch, LRU_BLOCK_DIM), F32),
            pltpu.VMEM((LRU_BLOCKS, B * pitch, LRU_BLOCK_DIM), F32),
            pltpu.VMEM((LRU_BLOCKS, B, LRU_BLOCK_DIM), F32),
        ],
        compiler_params=_params("arbitrary"),
        name="rglru",
    )(h3, h3, conv_w, conv_b, wg, ba, bx, lam)
    return out.reshape(T, W)


def _kv_proj_kernel(m_ref, g_ref, wk_ref, wv_ref, k_ref, v_ref, mn_ref):
    @pl.when(pl.program_id(0) == 0)
    def _():
        mn_ref[...] = _rms(m_ref[...], g_ref[...]).astype(BF16)

    mn = mn_ref[...]
    k_ref[...] = jnp.dot(mn, wk_ref[...], preferred_element_type=F32).astype(BF16)
    v_ref[...] = jnp.dot(mn, wv_ref[...], preferred_element_type=F32).astype(BF16)


def _kv_proj(mem, g, wk, wv, tn):
    M, D = mem.shape
    N = wk.shape[1]
    return pl.pallas_call(
        _kv_proj_kernel,
        out_shape=(jax.ShapeDtypeStruct((M, N), BF16), jax.ShapeDtypeStruct((M, N), BF16)),
        grid=(N // tn,),
        in_specs=[
            pl.BlockSpec((M, D), lambda j: (0, 0)),
            pl.BlockSpec((1, D), lambda j: (0, 0)),
            pl.BlockSpec((D, tn), lambda j: (0, j)),
            pl.BlockSpec((D, tn), lambda j: (0, j)),
        ],
        out_specs=(pl.BlockSpec((M, tn), lambda j: (0, j)), pl.BlockSpec((M, tn), lambda j: (0, j))),
        scratch_shapes=[pltpu.VMEM((M, D), BF16)],
        compiler_params=_params("arbitrary"),
        name="kv_proj",
    )(mem, g, wk, wv)


def _mid_kernel(x_ref, ret_ref, lru_ref, wout_ref, g2_ref, wq_ref, k_ref, v_ref, wo_ref, o_ref, att_ref):
    R = ret_ref.shape[1]
    x1 = (x_ref[...]
          + jnp.dot(ret_ref[...], wout_ref[0:R, :], preferred_element_type=F32)
          + jnp.dot(lru_ref[...], wout_ref[R:, :], preferred_element_type=F32))
    inv_rms = lax.rsqrt(jnp.mean(x1 * x1, axis=-1, keepdims=True) + EPS)
    q = (jnp.dot((x1 * g2_ref[...]).astype(BF16), wq_ref[...], preferred_element_type=F32) * inv_rms).astype(BF16)
    D = q.shape[1]
    dh = D // XA_HEADS
    scale = dh ** -0.5
    for hd in range(XA_HEADS):
        sl = slice(hd * dh, (hd + 1) * dh)
        s = lax.dot_general(q[:, sl], k_ref[:, sl], (((1,), (1,)), ((), ())),
                            preferred_element_type=F32) * scale
        e = jnp.exp(s - jnp.max(s, axis=-1, keepdims=True))
        o = jnp.dot(e.astype(BF16), v_ref[:, sl], preferred_element_type=F32)
        att_ref[:, sl] = (o / jnp.sum(e, axis=-1, keepdims=True)).astype(BF16)
    o_ref[...] = x1 + jnp.dot(att_ref[...], wo_ref[...], preferred_element_type=F32)


def _mid(x, ret, lru, w_out, g2, wq, kmem, vmem, wo, B, S, tm):
    T, D = x.shape
    R = ret.shape[1]
    M = kmem.shape[0] // B
    per_b = S // tm
    return pl.pallas_call(
        _mid_kernel,
        out_shape=jax.ShapeDtypeStruct((T, D), F32),
        grid=(T // tm,),
        in_specs=[
            pl.BlockSpec((tm, D), lambda i: (i, 0)),
            pl.BlockSpec((tm, R), lambda i: (i, 0)),
            pl.BlockSpec((tm, R), lambda i: (i, 0)),
            _resident((D, D)),
            pl.BlockSpec((1, D), lambda i: (0, 0)),
            _resident((D, D)),
            pl.BlockSpec((M, D), lambda i: (i // per_b, 0)),
            pl.BlockSpec((M, D), lambda i: (i // per_b, 0)),
            _resident((D, D)),
        ],
        out_specs=pl.BlockSpec((tm, D), lambda i: (i, 0)),
        scratch_shapes=[pltpu.VMEM((tm, D), BF16)],
        compiler_params=_params("arbitrary"),
        name="mid",
    )(x, ret, lru, w_out, g2, wq, kmem, vmem, wo)


def _ffn_kernel(x_ref, g3_ref, wa_ref, wb_ref, cwa_ref, cwb_ref, cba_ref, cbb_ref, wd_ref, gf_ref,
                o_ref, xn_ref, acc_ref, tail_ref, *hbuf_refs, tiles_per_seq):
    i = pl.program_id(0)
    j = pl.program_id(1)
    nj = pl.num_programs(1)
    tm = x_ref.shape[0]
    halo = V7X_SUBLANES
    n_sub = wd_ref.shape[0] // FFN_SUB

    @pl.when(j == 0)
    def _():
        xn_ref[...] = _rms(x_ref[...], g3_ref[...]).astype(BF16)
        acc_ref[...] = jnp.zeros_like(acc_ref)

    @pl.when(jnp.logical_and(j == 0, (i % tiles_per_seq) == 0))
    def _():
        tail_ref[...] = jnp.zeros_like(tail_ref)

    xn = xn_ref[...]

    def up(c, half, w_ref):
        slab = half * n_sub + c
        h = jnp.dot(xn, w_ref[:, c * FFN_SUB:(c + 1) * FFN_SUB], preferred_element_type=F32)
        hbuf_refs[slab][halo:halo + tm, :] = h
        hbuf_refs[slab][0:halo, :] = tail_ref[j, slab]
        tail_ref[j, slab] = h[tm - halo:tm, :]

    def conv(c, half, cw_ref, cb_ref):
        cs = slice(c * FFN_SUB, (c + 1) * FFN_SUB)
        hbuf_ref = hbuf_refs[half * n_sub + c]
        y = cb_ref[:, cs]
        for kk in range(FFN_CONV):
            off = halo - (FFN_CONV - 1) + kk
            y = y + cw_ref[kk:kk + 1, cs] * hbuf_ref[off:off + tm, :]
        return y

    for c in range(n_sub):
        up(c, 0, wa_ref)
        up(c, 1, wb_ref)
    part = None
    for c in range(n_sub):
        ya = conv(c, 0, cwa_ref, cba_ref)
        yb = conv(c, 1, cwb_ref, cbb_ref)
        gated = (ya * _sigmoid(ya) * yb).astype(BF16)
        d = jnp.dot(gated, wd_ref[c * FFN_SUB:(c + 1) * FFN_SUB, :], preferred_element_type=F32)
        part = d if part is None else part + d
    acc_ref[...] += part

    @pl.when(j == nj - 1)
    def _():
        o_ref[...] = _rms(x_ref[...] + acc_ref[...], gf_ref[...])


def _ffn(x, g3, w_up, conv_w, conv_b, w_down, gf, S, tm, tf):
    T, D = x.shape
    F = w_down.shape[0]
    nj = F // tf
    n_slabs = 2 * (tf // FFN_SUB)
    kernel = functools.partial(_ffn_kernel, tiles_per_seq=S // tm)
    return pl.pallas_call(
        kernel,
        out_shape=jax.ShapeDtypeStruct((T, D), F32),
        grid=(T // tm, nj),
        in_specs=[
            pl.BlockSpec((tm, D), lambda i, j: (i, 0)),
            pl.BlockSpec((1, D), lambda i, j: (0, 0)),
            pl.BlockSpec((D, tf), lambda i, j: (0, j)),
            pl.BlockSpec((D, tf), lambda i, j: (0, j + nj)),
            pl.BlockSpec((FFN_CONV, tf), lambda i, j: (0, j)),
            pl.BlockSpec((FFN_CONV, tf), lambda i, j: (0, j + nj)),
            pl.BlockSpec((1, tf), lambda i, j: (0, j)),
            pl.BlockSpec((1, tf), lambda i, j: (0, j + nj)),
            pl.BlockSpec((tf, D), lambda i, j: (j, 0)),
            pl.BlockSpec((1, D), lambda i, j: (0, 0)),
        ],
        out_specs=pl.BlockSpec((tm, D), lambda i, j: (i, 0)),
        scratch_shapes=[
            pltpu.VMEM((tm, D), BF16),
            pltpu.VMEM((tm, D), F32),
            pltpu.VMEM((nj, n_slabs, V7X_SUBLANES, FFN_SUB), F32),
        ] + [pltpu.VMEM((tm + V7X_SUBLANES, FFN_SUB), F32)] * n_slabs,
        compiler_params=_params("arbitrary", "arbitrary"),
        name="ffn",
    )(x, g3, w_up, w_up, conv_w, conv_w, conv_b, conv_b, w_down, gf)


def _layer(x, mem, posf, inv, B, S, norm1_g, w_in, ret_g, rg_conv_w, rg_conv_b, rg_wa, rg_ba, rg_wx, rg_bx,
           rg_lambda, w_out, norm2_g, norm_mem_g, xa_wq, xa_wk, xa_wv, xa_wo, norm3_g, ffn_w_up,
           ffn_conv_w, ffn_conv_b, ffn_w_down, out_g, tiles):
    W = LRU_BLOCKS * LRU_BLOCK_DIM
    row = lambda a: a.reshape(1, -1)
    h = _in_proj(x, row(norm1_g), posf, inv, w_in.astype(BF16), tiles["in_tm"], tiles["ret_chunk"])
    ret = _retention(h, row(ret_g), B, S, tiles["ret_chunk"])
    wg = jnp.concatenate([rg_wa, rg_wx], axis=-1).astype(BF16)
    lru = _rglru(h, rg_conv_w, row(rg_conv_b), wg, rg_ba.reshape(1, W), rg_bx.reshape(1, W),
                 row(rg_lambda), B, S, tiles["lru_tt"])
    kmem, vmem = _kv_proj(mem, row(norm_mem_g), xa_wk.astype(BF16), xa_wv.astype(BF16), tiles["kv_tn"])
    x2 = _mid(x, ret, lru, w_out.astype(BF16), row(norm2_g), xa_wq.astype(BF16), kmem, vmem,
              xa_wo.astype(BF16), B, S, tiles["mid_tm"])
    return _ffn(x2, row(norm3_g), ffn_w_up.astype(BF16), ffn_conv_w, row(ffn_conv_b),
                ffn_w_down.astype(BF16), out_g, S, tiles["ffn_tm"], tiles["ffn_tf"])


_TILES = dict(in_tm=IN_TM, ret_chunk=RET_CHUNK, lru_tt=LRU_TT, kv_tn=KV_TN, mid_tm=MID_TM,
              ffn_tm=FFN_TM, ffn_tf=FFN_TF)


def _block(x, mem, positions, norm1_g, w_in, ret_g, rg_conv_w, rg_conv_b, rg_wa, rg_ba, rg_wx, rg_bx,
           rg_lambda, w_out, norm2_g, norm_mem_g, xa_wq, xa_wk, xa_wv, xa_wo, norm3_g, ffn_w_up,
           ffn_conv_w, ffn_conv_b, ffn_w_down, final_g, tiles):
    B, S, D = x.shape
    assert w_in.shape[0] == 1, "only depth 1 is supported"
    T = B * S
    half = RET_HEAD_DIM // 2
    inv = (ROPE_BASE ** (-jnp.arange(half, dtype=F32) / half)).reshape(1, half)
    posf = jnp.broadcast_to(positions.reshape(T, 1).astype(F32), (T, half))
    out = _layer(x.reshape(T, D), mem.reshape(B * mem.shape[1], D), posf, inv, B, S, norm1_g[0], w_in[0],
                 ret_g[0], rg_conv_w[0], rg_conv_b[0], rg_wa[0], rg_ba[0], rg_wx[0], rg_bx[0], rg_lambda[0],
                 w_out[0], norm2_g[0], norm_mem_g[0], xa_wq[0], xa_wk[0], xa_wv[0], xa_wo[0], norm3_g[0],
                 ffn_w_up[0], ffn_conv_w[0], ffn_conv_b[0], ffn_w_down[0], final_g.reshape(1, D), tiles)
    return out.reshape(B, S, D)


def kernel(x, mem, positions, norm1_g, w_in, ret_g, rg_conv_w, rg_conv_b, rg_wa, rg_ba, rg_wx, rg_bx, rg_lambda, w_out, norm2_g, norm_mem_g, xa_wq, xa_wk, xa_wv, xa_wo, norm3_g, ffn_w_up, ffn_conv_w, ffn_conv_b, ffn_w_down, final_g):
    return _block(x, mem, positions, norm1_g, w_in, ret_g, rg_conv_w, rg_conv_b, rg_wa, rg_ba, rg_wx, rg_bx,
                  rg_lambda, w_out, norm2_g, norm_mem_g, xa_wq, xa_wk, xa_wv, xa_wo, norm3_g, ffn_w_up,
                  ffn_conv_w, ffn_conv_b, ffn_w_down, final_g, _TILES)
```

```python
import functools
import math

import jax
import jax.numpy as jnp
from jax import lax
from jax.experimental import pallas as pl
from jax.experimental.pallas import tpu as pltpu

F32 = jnp.float32
BF16 = jnp.bfloat16

EPS = 1e-6
ROPE_BASE = 10000.0
RET_HEADS = 4
RET_HEAD_DIM = 256
LRU_BLOCKS = 8
LRU_BLOCK_DIM = 128
LRU_CONV = 4
LRU_C = 8.0
XA_HEADS = 4
FFN_CONV = 3

V7X_LANES = 128
V7X_SUBLANES = 8
V7X_F32_TINY = 1.1754944e-38
V7X_VMEM_LIMIT = 56 * 1024 * 1024

IN_TM = 512
RET_CHUNK = 256
RET_CHUNKS_PER_STEP = 4
LRU_TT = 256
LRU_UNROLL = 8
MID_TM = 512
MID_SUB = 256
FFN_TM = 512
FFN_TF = 512
FFN_SUB = 256
KV_TN = 512


def _params(*sem):
    return pltpu.CompilerParams(dimension_semantics=sem, vmem_limit_bytes=V7X_VMEM_LIMIT)


def _rms(x, g):
    return x * lax.rsqrt(jnp.mean(x * x, axis=-1, keepdims=True) + EPS) * g


def _sigmoid(x):
    return 1.0 / (1.0 + jnp.exp(-x))


def _resident(shape):
    nd = len(shape)
    return pl.BlockSpec(shape, lambda *_: (0,) * nd, pipeline_mode=pl.Buffered(1))


def _in_proj_kernel(x_ref, g_ref, pos_ref, inv_ref, dec_ref, w_ref, o_ref, xn_ref, *, chunk):
    tm = x_ref.shape[0]
    tn = RET_HEADS * RET_HEAD_DIM
    half = RET_HEAD_DIM // 2
    xn_ref[...] = _rms(x_ref[...], g_ref[...]).astype(BF16)
    xn = xn_ref[...]
    n_groups = o_ref.shape[1] // tn
    cos = sin = None
    for grp in list(range(2, n_groups)) + [0, 1]:
        cs = slice(grp * tn, (grp + 1) * tn)
        h = jnp.dot(xn, w_ref[:, cs], preferred_element_type=F32)
        if cos is None:
            ang = pos_ref[...] * inv_ref[...]
            cos = jnp.cos(ang)
            sin = jnp.sin(ang)
        if grp == 3:
            h = h * _sigmoid(h)
        if grp == 5:
            h = 0.5 * h * (1.0 + jnp.tanh(math.sqrt(2.0 / math.pi) * (h + 0.044715 * (h * h * h))))
        if grp >= 2:
            o_ref[:, cs] = h.astype(BF16)
            continue
        for r0 in range(0, tm, chunk):
            rs = slice(r0, r0 + chunk)
            for hd in range(RET_HEADS):
                d = dec_ref[grp, :, hd * half:(hd + 1) * half]
                c = cos[rs] * d
                s = sin[rs] * d
                lo = hd * RET_HEAD_DIM
                t1 = h[rs, lo:lo + half]
                t2 = h[rs, lo + half:lo + RET_HEAD_DIM]
                o_ref[rs, grp * tn + lo:grp * tn + lo + half] = (t1 * c - t2 * s).astype(BF16)
                o_ref[rs, grp * tn + lo + half:grp * tn + lo + RET_HEAD_DIM] = (t1 * s + t2 * c).astype(BF16)


def _retention_log_gamma():
    return jnp.log(1.0 - 2.0 ** (-5.0 - jnp.arange(RET_HEADS, dtype=F32)))


def _in_proj(x, g, posf, inv, w, tm, C):
    T, D = x.shape
    N = w.shape[1]
    half = RET_HEAD_DIM // 2
    assert tm % C == 0
    t = jnp.arange(C, dtype=F32)
    log_g = _retention_log_gamma()
    q_dec = jnp.exp(log_g[None, :] * (t[:, None] + 1.0))
    k_dec = jnp.exp(log_g[None, :] * (C - 1.0 - t[:, None])) * (RET_HEAD_DIM ** -0.5)
    dec = jnp.repeat(jnp.stack([q_dec, k_dec]), half, axis=-1)
    return pl.pallas_call(
        functools.partial(_in_proj_kernel, chunk=C),
        out_shape=jax.ShapeDtypeStruct((T, N), BF16),
        grid=(T // tm,),
        in_specs=[
            pl.BlockSpec((tm, D), lambda i: (i, 0)),
            pl.BlockSpec((1, D), lambda i: (0, 0)),
            pl.BlockSpec((tm, half), lambda i: (i, 0)),
            pl.BlockSpec((1, half), lambda i: (0, 0)),
            _resident((2, C, RET_HEADS * half)),
            _resident((D, N)),
        ],
        out_specs=pl.BlockSpec((tm, N), lambda i: (i, 0)),
        scratch_shapes=[pltpu.VMEM((tm, D), BF16)],
        compiler_params=_params("arbitrary"),
        name="in_proj",
    )(x, g, posf, inv, dec, w)


def _retention_kernel(q_ref, k_ref, v_ref, g_ref, mask_ref, cdec_ref, rg_ref, o_ref, state_ref):
    n = pl.program_id(1)

    @pl.when(n == 0)
    def _():
        state_ref[...] = jnp.zeros_like(state_ref)

    dh = RET_HEAD_DIM
    C = mask_ref.shape[1]
    for r0 in range(0, q_ref.shape[0], C):
        rs = slice(r0, r0 + C)
        for hd in range(RET_HEADS):
            sl = slice(hd * dh, (hd + 1) * dh)
            q = q_ref[rs, sl]
            k = k_ref[rs, sl]
            v = v_ref[rs, sl]
            state = state_ref[hd]
            s = lax.dot_general(q, k, (((1,), (1,)), ((), ())), preferred_element_type=F32) * mask_ref[hd]
            ret = (jnp.dot(s.astype(BF16), v, preferred_element_type=F32)
                   + jnp.dot(q, state.astype(BF16), preferred_element_type=F32))
            state_ref[hd] = state * cdec_ref[hd] + lax.dot_general(
                k, v, (((0,), (0,)), ((), ())), preferred_element_type=F32)
            ret = ret * lax.rsqrt(jnp.mean(ret * ret, axis=-1, keepdims=True) + EPS)
            o_ref[rs, sl] = (ret * rg_ref[:, sl] * g_ref[rs, sl].astype(F32)).astype(BF16)


def _retention(h, ret_g, B, S, C):
    T = h.shape[0]
    R = RET_HEADS * RET_HEAD_DIM
    rows = C * RET_CHUNKS_PER_STEP
    N = S // rows
    H, dh = RET_HEADS, RET_HEAD_DIM
    log_g = _retention_log_gamma()
    idx = jnp.arange(C)
    mask = jnp.where(idx[:, None] >= idx[None, :], jnp.exp(-log_g * C)[:, None, None], 0.0)
    c_dec = jnp.broadcast_to(jnp.exp(log_g * C)[:, None, None], (H, 1, dh))

    def col(c):
        return pl.BlockSpec((rows, R), lambda b, n: (b * N + n, c))

    return pl.pallas_call(
        _retention_kernel,
        out_shape=jax.ShapeDtypeStruct((T, R), BF16),
        grid=(B, N),
        in_specs=[
            col(0), col(1), col(2), col(3),
            pl.BlockSpec((H, C, C), lambda b, n: (0, 0, 0)),
            pl.BlockSpec((H, 1, dh), lambda b, n: (0, 0, 0)),
            pl.BlockSpec((1, R), lambda b, n: (0, 0)),
        ],
        out_specs=pl.BlockSpec((rows, R), lambda b, n: (b * N + n, 0)),
        scratch_shapes=[pltpu.VMEM((H, dh, dh), F32)],
        compiler_params=_params("arbitrary", "arbitrary"),
        name="retention",
    )(h, h, h, h, mask, c_dec, ret_g)


def _rglru_kernel(u_ref, y_ref, cw_ref, cb_ref, wg_ref, ba_ref, bx_ref, lam_ref,
                  o_ref, ubuf_ref, utail_ref, a_ref, b_ref, hcar_ref):
    n = pl.program_id(0)
    B, tt, W = u_ref.shape
    halo = V7X_SUBLANES
    bd = LRU_BLOCK_DIM
    pitch = a_ref.shape[1] // B

    @pl.when(n == 0)
    def _():
        utail_ref[...] = jnp.zeros_like(utail_ref)
        hcar_ref[...] = jnp.zeros_like(hcar_ref)

    lam = lam_ref[...]
    log_sig = jnp.minimum(lam, 0.0) - jnp.log1p(jnp.exp(-jnp.abs(lam)))
    for s in range(B):
        ubuf_ref[0:halo, :] = utail_ref[s]
        ubuf_ref[halo:halo + tt, :] = u_ref[s].astype(F32)
        utail_ref[s] = ubuf_ref[tt:tt + halo, :]
        ext = ubuf_ref[...]
        uc = cw_ref[0:1, :] * ext
        for kk in range(1, LRU_CONV):
            uc = cw_ref[kk:kk + 1, :] * ext + pltpu.roll(uc, 1, 0)
        uc = uc[halo:halo + tt, :] + cb_ref[...]
        for nb in range(LRU_BLOCKS):
            sl = slice(nb * bd, (nb + 1) * bd)
            ub = uc[:, sl]
            z = jnp.dot(ub.astype(BF16), wg_ref[nb], preferred_element_type=F32)
            r = _sigmoid(z[:, :bd] + ba_ref[:, sl])
            ig = _sigmoid(z[:, bd:] + bx_ref[:, sl])
            log_a = LRU_C * r * log_sig[:, sl]
            a_ref[nb, s * pitch:s * pitch + tt, :] = jnp.exp(log_a)
            th = jnp.tanh(log_a)
            q = -2.0 * th / (1.0 - th)
            b_ref[nb, s * pitch:s * pitch + tt, :] = q * lax.rsqrt(jnp.maximum(q, V7X_F32_TINY)) * (ig * ub)

    def step(t, hs):
        out = []
        for nb in range(LRU_BLOCKS):
            rows = pl.ds(t, B, stride=pitch)
            hn = a_ref[nb, rows, :] * hs[nb] + b_ref[nb, rows, :]
            b_ref[nb, rows, :] = hn
            out.append(hn)
        return tuple(out)

    h0 = tuple(hcar_ref[nb] for nb in range(LRU_BLOCKS))
    hT = lax.fori_loop(0, tt, step, h0, unroll=LRU_UNROLL)
    for nb in range(LRU_BLOCKS):
        hcar_ref[nb] = hT[nb]

    for s in range(B):
        for nb in range(LRU_BLOCKS):
            sl = slice(nb * bd, (nb + 1) * bd)
            o_ref[s, :, sl] = (b_ref[nb, s * pitch:s * pitch + tt, :] * y_ref[s, :, sl].astype(F32)).astype(BF16)


def _rglru(h, conv_w, conv_b, wg, ba, bx, lam, B, S, tt):
    T = h.shape[0]
    W = LRU_BLOCKS * LRU_BLOCK_DIM
    h3 = h.reshape(B, S, h.shape[1])
    pitch = tt + V7X_SUBLANES
    assert B <= 4 and tt % (4 * V7X_SUBLANES) == 0

    def const(shape):
        nd = len(shape)
        return pl.BlockSpec(shape, lambda n: (0,) * nd)

    out = pl.pallas_call(
        _rglru_kernel,
        out_shape=jax.ShapeDtypeStruct((B, S, W), BF16),
        grid=(S // tt,),
        in_specs=[
            pl.BlockSpec((B, tt, W), lambda n: (0, n, 4)),
            pl.BlockSpec((B, tt, W), lambda n: (0, n, 5)),
            const((LRU_CONV, W)), const((1, W)),
            const((LRU_BLOCKS, LRU_BLOCK_DIM, 2 * LRU_BLOCK_DIM)),
            const((1, W)), const((1, W)), const((1, W)),
        ],
        out_specs=pl.BlockSpec((B, tt, W), lambda n: (0, n, 0)),
        scratch_shapes=[
            pltpu.VMEM((tt + V7X_SUBLANES, W), F32),
            pltpu.VMEM((B, V7X_SUBLANES, W), F32),
            pltpu.VMEM((LRU_BLOCKS, B * pitch, LRU_BLOCK_DIM), F32),
            pltpu.VMEM((LRU_BLOCKS, B * pitch, LRU_BLOCK_DIM), F32),
            pltpu.VMEM((LRU_BLOCKS, B, LRU_BLOCK_DIM), F32),
        ],
        compiler_params=_params("arbitrary"),
        name="rglru",
    )(h3, h3, conv_w, conv_b, wg, ba, bx, lam)
    return out.reshape(T, W)


def _kv_proj_kernel(m_ref, g_ref, wk_ref, wv_ref, k_ref, v_ref, mn_ref):
    @pl.when(pl.program_id(0) == 0)
    def _():
        mn_ref[...] = _rms(m_ref[...], g_ref[...]).astype(BF16)

    mn = mn_ref[...]
    k_ref[...] = jnp.dot(mn, wk_ref[...], preferred_element_type=F32).astype(BF16)
    v_ref[...] = jnp.dot(mn, wv_ref[...], preferred_element_type=F32).astype(BF16)


def _kv_proj(mem, g, wk, wv, tn):
    M, D = mem.shape
    N = wk.shape[1]
    return pl.pallas_call(
        _kv_proj_kernel,
        out_shape=(jax.ShapeDtypeStruct((M, N), BF16), jax.ShapeDtypeStruct((M, N), BF16)),
        grid=(N // tn,),
        in_specs=[
            pl.BlockSpec((M, D), lambda j: (0, 0)),
            pl.BlockSpec((1, D), lambda j: (0, 0)),
            pl.BlockSpec((D, tn), lambda j: (0, j)),
            pl.BlockSpec((D, tn), lambda j: (0, j)),
        ],
        out_specs=(pl.BlockSpec((M, tn), lambda j: (0, j)), pl.BlockSpec((M, tn), lambda j: (0, j))),
        scratch_shapes=[pltpu.VMEM((M, D), BF16)],
        compiler_params=_params("arbitrary"),
        name="kv_proj",
    )(mem, g, wk, wv)


def _mid_kernel(x_ref, ret_ref, lru_ref, wout_ref, g2_ref, wq_ref, k_ref, v_ref, wo_ref, o_ref, att_ref):
    R = ret_ref.shape[1]
    D = x_ref.shape[1]
    dh = D // XA_HEADS
    scale = dh ** -0.5
    for r0 in range(0, x_ref.shape[0], MID_SUB):
        rs = slice(r0, r0 + MID_SUB)
        x1 = (x_ref[rs, :]
              + jnp.dot(ret_ref[rs, :], wout_ref[0:R, :], preferred_element_type=F32)
              + jnp.dot(lru_ref[rs, :], wout_ref[R:, :], preferred_element_type=F32))
        inv_rms = lax.rsqrt(jnp.mean(x1 * x1, axis=-1, keepdims=True) + EPS)
        q = (jnp.dot((x1 * g2_ref[...]).astype(BF16), wq_ref[...], preferred_element_type=F32)
             * inv_rms).astype(BF16)
        for hd in range(XA_HEADS):
            sl = slice(hd * dh, (hd + 1) * dh)
            s = lax.dot_general(q[:, sl], k_ref[:, sl], (((1,), (1,)), ((), ())),
                                preferred_element_type=F32) * scale
            e = jnp.exp(s - jnp.max(s, axis=-1, keepdims=True))
            o = jnp.dot(e.astype(BF16), v_ref[:, sl], preferred_element_type=F32)
            att_ref[rs, sl] = (o / jnp.sum(e, axis=-1, keepdims=True)).astype(BF16)
        o_ref[rs, :] = x1 + jnp.dot(att_ref[rs, :], wo_ref[...], preferred_element_type=F32)


def _mid(x, ret, lru, w_out, g2, wq, kmem, vmem, wo, B, S, tm):
    T, D = x.shape
    R = ret.shape[1]
    M = kmem.shape[0] // B
    per_b = S // tm
    return pl.pallas_call(
        _mid_kernel,
        out_shape=jax.ShapeDtypeStruct((T, D), F32),
        grid=(T // tm,),
        in_specs=[
            pl.BlockSpec((tm, D), lambda i: (i, 0)),
            pl.BlockSpec((tm, R), lambda i: (i, 0)),
            pl.BlockSpec((tm, R), lambda i: (i, 0)),
            _resident((D, D)),
            pl.BlockSpec((1, D), lambda i: (0, 0)),
            _resident((D, D)),
            pl.BlockSpec((M, D), lambda i: (i // per_b, 0)),
            pl.BlockSpec((M, D), lambda i: (i // per_b, 0)),
            _resident((D, D)),
        ],
        out_specs=pl.BlockSpec((tm, D), lambda i: (i, 0)),
        scratch_shapes=[pltpu.VMEM((tm, D), BF16)],
        compiler_params=_params("arbitrary"),
        name="mid",
    )(x, ret, lru, w_out, g2, wq, kmem, vmem, wo)


def _ffn_kernel(x_ref, g3_ref, wa_ref, wb_ref, cwa_ref, cwb_ref, cba_ref, cbb_ref, wd_ref, gf_ref,
                o_ref, xn_ref, acc_ref, tail_ref, *hbuf_refs, tiles_per_seq):
    i = pl.program_id(0)
    j = pl.program_id(1)
    nj = pl.num_programs(1)
    tm = x_ref.shape[0]
    halo = V7X_SUBLANES
    n_sub = wd_ref.shape[0] // FFN_SUB

    @pl.when(j == 0)
    def _():
        xn_ref[...] = _rms(x_ref[...], g3_ref[...]).astype(BF16)
        acc_ref[...] = jnp.zeros_like(acc_ref)

    @pl.when(jnp.logical_and(j == 0, (i % tiles_per_seq) == 0))
    def _():
        tail_ref[...] = jnp.zeros_like(tail_ref)

    xn = xn_ref[...]

    def up(c, half, w_ref):
        slab = half * n_sub + c
        h = jnp.dot(xn, w_ref[:, c * FFN_SUB:(c + 1) * FFN_SUB], preferred_element_type=F32)
        hbuf_refs[slab][halo:halo + tm, :] = h
        hbuf_refs[slab][0:halo, :] = tail_ref[j, slab]
        tail_ref[j, slab] = h[tm - halo:tm, :]

    def conv(c, half, cw_ref, cb_ref):
        cs = slice(c * FFN_SUB, (c + 1) * FFN_SUB)
        hbuf_ref = hbuf_refs[half * n_sub + c]
        y = cb_ref[:, cs]
        for kk in range(FFN_CONV):
            off = halo - (FFN_CONV - 1) + kk
            y = y + cw_ref[kk:kk + 1, cs] * hbuf_ref[off:off + tm, :]
        return y

    for c in range(n_sub):
        up(c, 0, wa_ref)
        up(c, 1, wb_ref)
    part = None
    for c in range(n_sub):
        ya = conv(c, 0, cwa_ref, cba_ref)
        yb = conv(c, 1, cwb_ref, cbb_ref)
        gated = (ya * _sigmoid(ya) * yb).astype(BF16)
        d = jnp.dot(gated, wd_ref[c * FFN_SUB:(c + 1) * FFN_SUB, :], preferred_element_type=F32)
        part = d if part is None else part + d
    acc_ref[...] += part

    @pl.when(j == nj - 1)
    def _():
        o_ref[...] = _rms(x_ref[...] + acc_ref[...], gf_ref[...])


def _ffn(x, g3, w_up, conv_w, conv_b, w_down, gf, S, tm, tf):
    T, D = x.shape
    F = w_down.shape[0]
    nj = F // tf
    n_slabs = 2 * (tf // FFN_SUB)
    kernel = functools.partial(_ffn_kernel, tiles_per_seq=S // tm)
    return pl.pallas_call(
        kernel,
        out_shape=jax.ShapeDtypeStruct((T, D), F32),
        grid=(T // tm, nj),
        in_specs=[
            pl.BlockSpec((tm, D), lambda i, j: (i, 0)),
            pl.BlockSpec((1, D), lambda i, j: (0, 0)),
            pl.BlockSpec((D, tf), lambda i, j: (0, j)),
            pl.BlockSpec((D, tf), lambda i, j: (0, j + nj)),
            pl.BlockSpec((FFN_CONV, tf), lambda i, j: (0, j)),
            pl.BlockSpec((FFN_CONV, tf), lambda i, j: (0, j + nj)),
            pl.BlockSpec((1, tf), lambda i, j: (0, j)),
            pl.BlockSpec((1, tf), lambda i, j: (0, j + nj)),
            pl.BlockSpec((tf, D), lambda i, j: (j, 0)),
            pl.BlockSpec((1, D), lambda i, j: (0, 0)),
        ],
        out_specs=pl.BlockSpec((tm, D), lambda i, j: (i, 0)),
        scratch_shapes=[
            pltpu.VMEM((tm, D), BF16),
            pltpu.VMEM((tm, D), F32),
            pltpu.VMEM((nj, n_slabs, V7X_SUBLANES, FFN_SUB), F32),
        ] + [pltpu.VMEM((tm + V7X_SUBLANES, FFN_SUB), F32)] * n_slabs,
        compiler_params=_params("arbitrary", "arbitrary"),
        name="ffn",
    )(x, g3, w_up, w_up, conv_w, conv_w, conv_b, conv_b, w_down, gf)


def _layer(x, mem, posf, inv, B, S, norm1_g, w_in, ret_g, rg_conv_w, rg_conv_b, rg_wa, rg_ba, rg_wx, rg_bx,
           rg_lambda, w_out, norm2_g, norm_mem_g, xa_wq, xa_wk, xa_wv, xa_wo, norm3_g, ffn_w_up,
           ffn_conv_w, ffn_conv_b, ffn_w_down, out_g, tiles):
    W = LRU_BLOCKS * LRU_BLOCK_DIM
    row = lambda a: a.reshape(1, -1)
    h = _in_proj(x, row(norm1_g), posf, inv, w_in.astype(BF16), tiles["in_tm"], tiles["ret_chunk"])
    ret = _retention(h, row(ret_g), B, S, tiles["ret_chunk"])
    wg = jnp.concatenate([rg_wa, rg_wx], axis=-1).astype(BF16)
    lru = _rglru(h, rg_conv_w, row(rg_conv_b), wg, rg_ba.reshape(1, W), rg_bx.reshape(1, W),
                 row(rg_lambda), B, S, tiles["lru_tt"])
    kmem, vmem = _kv_proj(mem, row(norm_mem_g), xa_wk.astype(BF16), xa_wv.astype(BF16), tiles["kv_tn"])
    x2 = _mid(x, ret, lru, w_out.astype(BF16), row(norm2_g), xa_wq.astype(BF16), kmem, vmem,
              xa_wo.astype(BF16), B, S, tiles["mid_tm"])
    return _ffn(x2, row(norm3_g), ffn_w_up.astype(BF16), ffn_conv_w, row(ffn_conv_b),
                ffn_w_down.astype(BF16), out_g, S, tiles["ffn_tm"], tiles["ffn_tf"])


_TILES = dict(in_tm=IN_TM, ret_chunk=RET_CHUNK, lru_tt=LRU_TT, kv_tn=KV_TN, mid_tm=MID_TM,
              ffn_tm=FFN_TM, ffn_tf=FFN_TF)


def _block(x, mem, positions, norm1_g, w_in, ret_g, rg_conv_w, rg_conv_b, rg_wa, rg_ba, rg_wx, rg_bx,
           rg_lambda, w_out, norm2_g, norm_mem_g, xa_wq, xa_wk, xa_wv, xa_wo, norm3_g, ffn_w_up,
           ffn_conv_w, ffn_conv_b, ffn_w_down, final_g, tiles):
    B, S, D = x.shape
    assert w_in.shape[0] == 1, "only depth 1 is supported"
    T = B * S
    half = RET_HEAD_DIM // 2
    inv = (ROPE_BASE ** (-jnp.arange(half, dtype=F32) / half)).reshape(1, half)
    posf = jnp.broadcast_to(positions.reshape(T, 1).astype(F32), (T, half))
    out = _layer(x.reshape(T, D), mem.reshape(B * mem.shape[1], D), posf, inv, B, S, norm1_g[0], w_in[0],
                 ret_g[0], rg_conv_w[0], rg_conv_b[0], rg_wa[0], rg_ba[0], rg_wx[0], rg_bx[0], rg_lambda[0],
                 w_out[0], norm2_g[0], norm_mem_g[0], xa_wq[0], xa_wk[0], xa_wv[0], xa_wo[0], norm3_g[0],
                 ffn_w_up[0], ffn_conv_w[0], ffn_conv_b[0], ffn_w_down[0], final_g.reshape(1, D), tiles)
    return out.reshape(B, S, D)


def kernel(x, mem, positions, norm1_g, w_in, ret_g, rg_conv_w, rg_conv_b, rg_wa, rg_ba, rg_wx, rg_bx, rg_lambda, w_out, norm2_g, norm_mem_g, xa_wq, xa_wk, xa_wv, xa_wo, norm3_g, ffn_w_up, ffn_conv_w, ffn_conv_b, ffn_w_down, final_g):
    return _block(x, mem, positions, norm1_g, w_in, ret_g, rg_conv_w, rg_conv_b, rg_wa, rg_ba, rg_wx, rg_bx,
                  rg_lambda, w_out, norm2_g, norm_mem_g, xa_wq, xa_wk, xa_wv, xa_wo, norm3_g, ffn_w_up,
                  ffn_conv_w, ffn_conv_b, ffn_w_down, final_g, _TILES)
```

```python
import functools
import math

import jax
import jax.numpy as jnp
from jax import lax
from jax.experimental import pallas as pl
from jax.experimental.pallas import tpu as pltpu

F32 = jnp.float32
BF16 = jnp.bfloat16

EPS = 1e-6
ROPE_BASE = 10000.0
RET_HEADS = 4
RET_HEAD_DIM = 256
LRU_BLOCKS = 8
LRU_BLOCK_DIM = 128
LRU_CONV = 4
LRU_C = 8.0
XA_HEADS = 4
FFN_CONV = 3

V7X_LANES = 128
V7X_SUBLANES = 8
V7X_F32_TINY = 1.1754944e-38
V7X_VMEM_LIMIT = 56 * 1024 * 1024

IN_TM = 512
RET_CHUNK = 256
RET_CHUNKS_PER_STEP = 4
LRU_TT = 256
LRU_UNROLL = 8
MID_TM = 512
MID_SUB = 256
FFN_TM = 512
FFN_TF = 512
FFN_ROWS = 256
FFN_SUB = 256
KV_TN = 512


def _params(*sem):
    return pltpu.CompilerParams(dimension_semantics=sem, vmem_limit_bytes=V7X_VMEM_LIMIT)


def _rms(x, g):
    return x * lax.rsqrt(jnp.mean(x * x, axis=-1, keepdims=True) + EPS) * g


def _sigmoid(x):
    return 1.0 / (1.0 + jnp.exp(-x))


def _resident(shape):
    nd = len(shape)
    return pl.BlockSpec(shape, lambda *_: (0,) * nd, pipeline_mode=pl.Buffered(1))


def _in_proj_kernel(x_ref, g_ref, pos_ref, inv_ref, dec_ref, w_ref, o_ref, xn_ref, *, chunk):
    tm = x_ref.shape[0]
    tn = RET_HEADS * RET_HEAD_DIM
    half = RET_HEAD_DIM // 2
    xn_ref[...] = _rms(x_ref[...], g_ref[...]).astype(BF16)
    xn = xn_ref[...]
    n_groups = o_ref.shape[1] // tn
    cos = sin = None
    for grp in list(range(2, n_groups)) + [0, 1]:
        cs = slice(grp * tn, (grp + 1) * tn)
        h = jnp.dot(xn, w_ref[:, cs], preferred_element_type=F32)
        if cos is None:
            ang = pos_ref[...] * inv_ref[...]
            cos = jnp.cos(ang)
            sin = jnp.sin(ang)
        if grp == 3:
            h = h * _sigmoid(h)
        if grp == 5:
            h = 0.5 * h * (1.0 + jnp.tanh(math.sqrt(2.0 / math.pi) * (h + 0.044715 * (h * h * h))))
        if grp >= 2:
            o_ref[:, cs] = h.astype(BF16)
            continue
        for r0 in range(0, tm, chunk):
            rs = slice(r0, r0 + chunk)
            for hd in range(RET_HEADS):
                d = dec_ref[grp, :, hd * half:(hd + 1) * half]
                c = cos[rs] * d
                s = sin[rs] * d
                lo = hd * RET_HEAD_DIM
                t1 = h[rs, lo:lo + half]
                t2 = h[rs, lo + half:lo + RET_HEAD_DIM]
                o_ref[rs, grp * tn + lo:grp * tn + lo + half] = (t1 * c - t2 * s).astype(BF16)
                o_ref[rs, grp * tn + lo + half:grp * tn + lo + RET_HEAD_DIM] = (t1 * s + t2 * c).astype(BF16)


def _retention_log_gamma():
    return jnp.log(1.0 - 2.0 ** (-5.0 - jnp.arange(RET_HEADS, dtype=F32)))


def _in_proj(x, g, posf, inv, w, tm, C):
    T, D = x.shape
    N = w.shape[1]
    half = RET_HEAD_DIM // 2
    assert tm % C == 0
    t = jnp.arange(C, dtype=F32)
    log_g = _retention_log_gamma()
    q_dec = jnp.exp(log_g[None, :] * (t[:, None] + 1.0))
    k_dec = jnp.exp(log_g[None, :] * (C - 1.0 - t[:, None])) * (RET_HEAD_DIM ** -0.5)
    dec = jnp.repeat(jnp.stack([q_dec, k_dec]), half, axis=-1)
    return pl.pallas_call(
        functools.partial(_in_proj_kernel, chunk=C),
        out_shape=jax.ShapeDtypeStruct((T, N), BF16),
        grid=(T // tm,),
        in_specs=[
            pl.BlockSpec((tm, D), lambda i: (i, 0)),
            pl.BlockSpec((1, D), lambda i: (0, 0)),
            pl.BlockSpec((tm, half), lambda i: (i, 0)),
            pl.BlockSpec((1, half), lambda i: (0, 0)),
            _resident((2, C, RET_HEADS * half)),
            _resident((D, N)),
        ],
        out_specs=pl.BlockSpec((tm, N), lambda i: (i, 0)),
        scratch_shapes=[pltpu.VMEM((tm, D), BF16)],
        compiler_params=_params("arbitrary"),
        name="in_proj",
    )(x, g, posf, inv, dec, w)


def _retention_kernel(q_ref, k_ref, v_ref, g_ref, mask_ref, cdec_ref, rg_ref, o_ref, state_ref):
    n = pl.program_id(1)

    @pl.when(n == 0)
    def _():
        state_ref[...] = jnp.zeros_like(state_ref)

    dh = RET_HEAD_DIM
    C = mask_ref.shape[1]
    for r0 in range(0, q_ref.shape[0], C):
        rs = slice(r0, r0 + C)
        for hd in range(RET_HEADS):
            sl = slice(hd * dh, (hd + 1) * dh)
            q = q_ref[rs, sl]
            k = k_ref[rs, sl]
            v = v_ref[rs, sl]
            state = state_ref[hd]
            s = lax.dot_general(q, k, (((1,), (1,)), ((), ())), preferred_element_type=F32) * mask_ref[hd]
            ret = (jnp.dot(s.astype(BF16), v, preferred_element_type=F32)
                   + jnp.dot(q, state.astype(BF16), preferred_element_type=F32))
            state_ref[hd] = state * cdec_ref[hd] + lax.dot_general(
                k, v, (((0,), (0,)), ((), ())), preferred_element_type=F32)
            ret = ret * lax.rsqrt(jnp.mean(ret * ret, axis=-1, keepdims=True) + EPS)
            o_ref[rs, sl] = (ret * rg_ref[:, sl] * g_ref[rs, sl].astype(F32)).astype(BF16)


def _retention(h, ret_g, B, S, C):
    T = h.shape[0]
    R = RET_HEADS * RET_HEAD_DIM
    rows = C * RET_CHUNKS_PER_STEP
    N = S // rows
    H, dh = RET_HEADS, RET_HEAD_DIM
    log_g = _retention_log_gamma()
    idx = jnp.arange(C)
    mask = jnp.where(idx[:, None] >= idx[None, :], jnp.exp(-log_g * C)[:, None, None], 0.0)
    c_dec = jnp.broadcast_to(jnp.exp(log_g * C)[:, None, None], (H, 1, dh))

    def col(c):
        return pl.BlockSpec((rows, R), lambda b, n: (b * N + n, c))

    return pl.pallas_call(
        _retention_kernel,
        out_shape=jax.ShapeDtypeStruct((T, R), BF16),
        grid=(B, N),
        in_specs=[
            col(0), col(1), col(2), col(3),
            pl.BlockSpec((H, C, C), lambda b, n: (0, 0, 0)),
            pl.BlockSpec((H, 1, dh), lambda b, n: (0, 0, 0)),
            pl.BlockSpec((1, R), lambda b, n: (0, 0)),
        ],
        out_specs=pl.BlockSpec((rows, R), lambda b, n: (b * N + n, 0)),
        scratch_shapes=[pltpu.VMEM((H, dh, dh), F32)],
        compiler_params=_params("arbitrary", "arbitrary"),
        name="retention",
    )(h, h, h, h, mask, c_dec, ret_g)


def _rglru_kernel(u_ref, y_ref, cw_ref, cb_ref, wg_ref, ba_ref, bx_ref, lam_ref,
                  o_ref, ubuf_ref, utail_ref, a_ref, b_ref, hcar_ref):
    n = pl.program_id(0)
    B, tt, W = u_ref.shape
    halo = V7X_SUBLANES
    bd = LRU_BLOCK_DIM
    pitch = a_ref.shape[1] // B

    @pl.when(n == 0)
    def _():
        utail_ref[...] = jnp.zeros_like(utail_ref)
        hcar_ref[...] = jnp.zeros_like(hcar_ref)

    lam = lam_ref[...]
    log_sig = jnp.minimum(lam, 0.0) - jnp.log1p(jnp.exp(-jnp.abs(lam)))
    for s in range(B):
        ubuf_ref[0:halo, :] = utail_ref[s]
        ubuf_ref[halo:halo + tt, :] = u_ref[s].astype(F32)
        utail_ref[s] = ubuf_ref[tt:tt + halo, :]
        ext = ubuf_ref[...]
        uc = cw_ref[0:1, :] * ext
        for kk in range(1, LRU_CONV):
            uc = cw_ref[kk:kk + 1, :] * ext + pltpu.roll(uc, 1, 0)
        uc = uc[halo:halo + tt, :] + cb_ref[...]
        for nb in range(LRU_BLOCKS):
            sl = slice(nb * bd, (nb + 1) * bd)
            ub = uc[:, sl]
            z = jnp.dot(ub.astype(BF16), wg_ref[nb], preferred_element_type=F32)
            r = _sigmoid(z[:, :bd] + ba_ref[:, sl])
            ig = _sigmoid(z[:, bd:] + bx_ref[:, sl])
            log_a = LRU_C * r * log_sig[:, sl]
            a_ref[nb, s * pitch:s * pitch + tt, :] = jnp.exp(log_a)
            th = jnp.tanh(log_a)
            q = -2.0 * th / (1.0 - th)
            b_ref[nb, s * pitch:s * pitch + tt, :] = q * lax.rsqrt(jnp.maximum(q, V7X_F32_TINY)) * (ig * ub)

    def step(t, hs):
        out = []
        for nb in range(LRU_BLOCKS):
            rows = pl.ds(t, B, stride=pitch)
            hn = a_ref[nb, rows, :] * hs[nb] + b_ref[nb, rows, :]
            b_ref[nb, rows, :] = hn
            out.append(hn)
        return tuple(out)

    h0 = tuple(hcar_ref[nb] for nb in range(LRU_BLOCKS))
    hT = lax.fori_loop(0, tt, step, h0, unroll=LRU_UNROLL)
    for nb in range(LRU_BLOCKS):
        hcar_ref[nb] = hT[nb]

    for s in range(B):
        for nb in range(LRU_BLOCKS):
            sl = slice(nb * bd, (nb + 1) * bd)
            o_ref[s, :, sl] = (b_ref[nb, s * pitch:s * pitch + tt, :] * y_ref[s, :, sl].astype(F32)).astype(BF16)


def _rglru(h, conv_w, conv_b, wg, ba, bx, lam, B, S, tt):
    T = h.shape[0]
    W = LRU_BLOCKS * LRU_BLOCK_DIM
    h3 = h.reshape(B, S, h.shape[1])
    pitch = tt + V7X_SUBLANES
    assert B <= 4 and tt % (4 * V7X_SUBLANES) == 0

    def const(shape):
        nd = len(shape)
        return pl.BlockSpec(shape, lambda n: (0,) * nd)

    out = pl.pallas_call(
        _rglru_kernel,
        out_shape=jax.ShapeDtypeStruct((B, S, W), BF16),
        grid=(S // tt,),
        in_specs=[
            pl.BlockSpec((B, tt, W), lambda n: (0, n, 4)),
            pl.BlockSpec((B, tt, W), lambda n: (0, n, 5)),
            const((LRU_CONV, W)), const((1, W)),
            const((LRU_BLOCKS, LRU_BLOCK_DIM, 2 * LRU_BLOCK_DIM)),
            const((1, W)), const((1, W)), const((1, W)),
        ],
        out_specs=pl.BlockSpec((B, tt, W), lambda n: (0, n, 0)),
        scratch_shapes=[
            pltpu.VMEM((tt + V7X_SUBLANES, W), F32),
            pltpu.VMEM((B, V7X_SUBLANES, W), F32),
            pltpu.VMEM((LRU_BLOCKS, B * pitch, LRU_BLOCK_DIM), F32),
            pltpu.VMEM((LRU_BLOCKS, B * pitch, LRU_BLOCK_DIM), F32),
            pltpu.VMEM((LRU_BLOCKS, B, LRU_BLOCK_DIM), F32),
        ],
        compiler_params=_params("arbitrary"),
        name="rglru",
    )(h3, h3, conv_w, conv_b, wg, ba, bx, lam)
    return out.reshape(T, W)


def _kv_proj_kernel(m_ref, g_ref, wk_ref, wv_ref, k_ref, v_ref, mn_ref):
    @pl.when(pl.program_id(0) == 0)
    def _():
        mn_ref[...] = _rms(m_ref[...], g_ref[...]).astype(BF16)

    mn = mn_ref[...]
    k_ref[...] = jnp.dot(mn, wk_ref[...], preferred_element_type=F32).astype(BF16)
    v_ref[...] = jnp.dot(mn, wv_ref[...], preferred_element_type=F32).astype(BF16)


def _kv_proj(mem, g, wk, wv, tn):
    M, D = mem.shape
    N = wk.shape[1]
    return pl.pallas_call(
        _kv_proj_kernel,
        out_shape=(jax.ShapeDtypeStruct((M, N), BF16), jax.ShapeDtypeStruct((M, N), BF16)),
        grid=(N // tn,),
        in_specs=[
            pl.BlockSpec((M, D), lambda j: (0, 0)),
            pl.BlockSpec((1, D), lambda j: (0, 0)),
            pl.BlockSpec((D, tn), lambda j: (0, j)),
            pl.BlockSpec((D, tn), lambda j: (0, j)),
        ],
        out_specs=(pl.BlockSpec((M, tn), lambda j: (0, j)), pl.BlockSpec((M, tn), lambda j: (0, j))),
        scratch_shapes=[pltpu.VMEM((M, D), BF16)],
        compiler_params=_params("arbitrary"),
        name="kv_proj",
    )(mem, g, wk, wv)


def _mid_kernel(x_ref, ret_ref, lru_ref, wout_ref, g2_ref, wq_ref, k_ref, v_ref, wo_ref, o_ref, att_ref):
    R = ret_ref.shape[1]
    D = x_ref.shape[1]
    dh = D // XA_HEADS
    scale = dh ** -0.5
    for r0 in range(0, x_ref.shape[0], MID_SUB):
        rs = slice(r0, r0 + MID_SUB)
        x1 = (x_ref[rs, :]
              + jnp.dot(ret_ref[rs, :], wout_ref[0:R, :], preferred_element_type=F32)
              + jnp.dot(lru_ref[rs, :], wout_ref[R:, :], preferred_element_type=F32))
        inv_rms = lax.rsqrt(jnp.mean(x1 * x1, axis=-1, keepdims=True) + EPS)
        q = (jnp.dot((x1 * g2_ref[...]).astype(BF16), wq_ref[...], preferred_element_type=F32)
             * inv_rms).astype(BF16)
        for hd in range(XA_HEADS):
            sl = slice(hd * dh, (hd + 1) * dh)
            s = lax.dot_general(q[:, sl], k_ref[:, sl], (((1,), (1,)), ((), ())),
                                preferred_element_type=F32) * scale
            e = jnp.exp(s - jnp.max(s, axis=-1, keepdims=True))
            o = jnp.dot(e.astype(BF16), v_ref[:, sl], preferred_element_type=F32)
            att_ref[rs, sl] = (o / jnp.sum(e, axis=-1, keepdims=True)).astype(BF16)
        o_ref[rs, :] = x1 + jnp.dot(att_ref[rs, :], wo_ref[...], preferred_element_type=F32)


def _mid(x, ret, lru, w_out, g2, wq, kmem, vmem, wo, B, S, tm):
    T, D = x.shape
    R = ret.shape[1]
    M = kmem.shape[0] // B
    per_b = S // tm
    return pl.pallas_call(
        _mid_kernel,
        out_shape=jax.ShapeDtypeStruct((T, D), F32),
        grid=(T // tm,),
        in_specs=[
            pl.BlockSpec((tm, D), lambda i: (i, 0)),
            pl.BlockSpec((tm, R), lambda i: (i, 0)),
            pl.BlockSpec((tm, R), lambda i: (i, 0)),
            _resident((D, D)),
            pl.BlockSpec((1, D), lambda i: (0, 0)),
            _resident((D, D)),
            pl.BlockSpec((M, D), lambda i: (i // per_b, 0)),
            pl.BlockSpec((M, D), lambda i: (i // per_b, 0)),
            _resident((D, D)),
        ],
        out_specs=pl.BlockSpec((tm, D), lambda i: (i, 0)),
        scratch_shapes=[pltpu.VMEM((tm, D), BF16)],
        compiler_params=_params("arbitrary"),
        name="mid",
    )(x, ret, lru, w_out, g2, wq, kmem, vmem, wo)


def _ffn_kernel(x_ref, g3_ref, wa_ref, wb_ref, cwa_ref, cwb_ref, cba_ref, cbb_ref, wd_ref, gf_ref,
                o_ref, xn_ref, acc_ref, tail_ref, *hbuf_refs, tiles_per_seq):
    i = pl.program_id(0)
    j = pl.program_id(1)
    nj = pl.num_programs(1)
    tm = x_ref.shape[0]
    halo = V7X_SUBLANES
    n_sub = wd_ref.shape[0] // FFN_SUB

    @pl.when(j == 0)
    def _():
        xn_ref[...] = _rms(x_ref[...], g3_ref[...]).astype(BF16)
        acc_ref[...] = jnp.zeros_like(acc_ref)

    @pl.when(jnp.logical_and(j == 0, (i % tiles_per_seq) == 0))
    def _():
        tail_ref[...] = jnp.zeros_like(tail_ref)

    RS = FFN_ROWS
    for slab in range(2 * n_sub):
        hbuf_refs[slab][0:halo, :] = tail_ref[j, slab]

    def up(r0, c, half, w_ref):
        slab = half * n_sub + c
        h = jnp.dot(xn_ref[r0:r0 + RS, :], w_ref[:, c * FFN_SUB:(c + 1) * FFN_SUB], preferred_element_type=F32)
        hbuf_refs[slab][halo + r0:halo + r0 + RS, :] = h

    def conv(r0, c, half, cw_ref, cb_ref):
        cs = slice(c * FFN_SUB, (c + 1) * FFN_SUB)
        hbuf_ref = hbuf_refs[half * n_sub + c]
        y = cb_ref[:, cs]
        for kk in range(FFN_CONV):
            off = halo - (FFN_CONV - 1) + kk + r0
            y = y + cw_ref[kk:kk + 1, cs] * hbuf_ref[off:off + RS, :]
        return y

    for r0 in range(0, tm, RS):
        for c in range(n_sub):
            up(r0, c, 0, wa_ref)
            up(r0, c, 1, wb_ref)
    for r0 in range(0, tm, RS):
        part = None
        for c in range(n_sub):
            ya = conv(r0, c, 0, cwa_ref, cba_ref)
            yb = conv(r0, c, 1, cwb_ref, cbb_ref)
            gated = (ya * _sigmoid(ya) * yb).astype(BF16)
            d = jnp.dot(gated, wd_ref[c * FFN_SUB:(c + 1) * FFN_SUB, :], preferred_element_type=F32)
            part = d if part is None else part + d
        acc_ref[r0:r0 + RS, :] += part
    for slab in range(2 * n_sub):
        tail_ref[j, slab] = hbuf_refs[slab][tm:tm + halo, :]

    @pl.when(j == nj - 1)
    def _():
        o_ref[...] = _rms(x_ref[...] + acc_ref[...], gf_ref[...])


def _ffn(x, g3, w_up, conv_w, conv_b, w_down, gf, S, tm, tf):
    T, D = x.shape
    F = w_down.shape[0]
    nj = F // tf
    n_slabs = 2 * (tf // FFN_SUB)
    kernel = functools.partial(_ffn_kernel, tiles_per_seq=S // tm)
    return pl.pallas_call(
        kernel,
        out_shape=jax.ShapeDtypeStruct((T, D), F32),
        grid=(T // tm, nj),
        in_specs=[
            pl.BlockSpec((tm, D), lambda i, j: (i, 0)),
            pl.BlockSpec((1, D), lambda i, j: (0, 0)),
            pl.BlockSpec((D, tf), lambda i, j: (0, j)),
            pl.BlockSpec((D, tf), lambda i, j: (0, j + nj)),
            pl.BlockSpec((FFN_CONV, tf), lambda i, j: (0, j)),
            pl.BlockSpec((FFN_CONV, tf), lambda i, j: (0, j + nj)),
            pl.BlockSpec((1, tf), lambda i, j: (0, j)),
            pl.BlockSpec((1, tf), lambda i, j: (0, j + nj)),
            pl.BlockSpec((tf, D), lambda i, j: (j, 0)),
            pl.BlockSpec((1, D), lambda i, j: (0, 0)),
        ],
        out_specs=pl.BlockSpec((tm, D), lambda i, j: (i, 0)),
        scratch_shapes=[
            pltpu.VMEM((tm, D), BF16),
            pltpu.VMEM((tm, D), F32),
            pltpu.VMEM((nj, n_slabs, V7X_SUBLANES, FFN_SUB), F32),
        ] + [pltpu.VMEM((tm + V7X_SUBLANES, FFN_SUB), F32)] * n_slabs,
        compiler_params=_params("arbitrary", "arbitrary"),
        name="ffn",
    )(x, g3, w_up, w_up, conv_w, conv_w, conv_b, conv_b, w_down, gf)


def _layer(x, mem, posf, inv, B, S, norm1_g, w_in, ret_g, rg_conv_w, rg_conv_b, rg_wa, rg_ba, rg_wx, rg_bx,
           rg_lambda, w_out, norm2_g, norm_mem_g, xa_wq, xa_wk, xa_wv, xa_wo, norm3_g, ffn_w_up,
           ffn_conv_w, ffn_conv_b, ffn_w_down, out_g, tiles):
    W = LRU_BLOCKS * LRU_BLOCK_DIM
    row = lambda a: a.reshape(1, -1)
    h = _in_proj(x, row(norm1_g), posf, inv, w_in.astype(BF16), tiles["in_tm"], tiles["ret_chunk"])
    ret = _retention(h, row(ret_g), B, S, tiles["ret_chunk"])
    wg = jnp.concatenate([rg_wa, rg_wx], axis=-1).astype(BF16)
    lru = _rglru(h, rg_conv_w, row(rg_conv_b), wg, rg_ba.reshape(1, W), rg_bx.reshape(1, W),
                 row(rg_lambda), B, S, tiles["lru_tt"])
    kmem, vmem = _kv_proj(mem, row(norm_mem_g), xa_wk.astype(BF16), xa_wv.astype(BF16), tiles["kv_tn"])
    x2 = _mid(x, ret, lru, w_out.astype(BF16), row(norm2_g), xa_wq.astype(BF16), kmem, vmem,
              xa_wo.astype(BF16), B, S, tiles["mid_tm"])
    return _ffn(x2, row(norm3_g), ffn_w_up.astype(BF16), ffn_conv_w, row(ffn_conv_b),
                ffn_w_down.astype(BF16), out_g, S, tiles["ffn_tm"], tiles["ffn_tf"])


_TILES = dict(in_tm=IN_TM, ret_chunk=RET_CHUNK, lru_tt=LRU_TT, kv_tn=KV_TN, mid_tm=MID_TM,
              ffn_tm=FFN_TM, ffn_tf=FFN_TF)


def _block(x, mem, positions, norm1_g, w_in, ret_g, rg_conv_w, rg_conv_b, rg_wa, rg_ba, rg_wx, rg_bx,
           rg_lambda, w_out, norm2_g, norm_mem_g, xa_wq, xa_wk, xa_wv, xa_wo, norm3_g, ffn_w_up,
           ffn_conv_w, ffn_conv_b, ffn_w_down, final_g, tiles):
    B, S, D = x.shape
    assert w_in.shape[0] == 1, "only depth 1 is supported"
    T = B * S
    half = RET_HEAD_DIM // 2
    inv = (ROPE_BASE ** (-jnp.arange(half, dtype=F32) / half)).reshape(1, half)
    posf = jnp.broadcast_to(positions.reshape(T, 1).astype(F32), (T, half))
    out = _layer(x.reshape(T, D), mem.reshape(B * mem.shape[1], D), posf, inv, B, S, norm1_g[0], w_in[0],
                 ret_g[0], rg_conv_w[0], rg_conv_b[0], rg_wa[0], rg_ba[0], rg_wx[0], rg_bx[0], rg_lambda[0],
                 w_out[0], norm2_g[0], norm_mem_g[0], xa_wq[0], xa_wk[0], xa_wv[0], xa_wo[0], norm3_g[0],
                 ffn_w_up[0], ffn_conv_w[0], ffn_conv_b[0], ffn_w_down[0], final_g.reshape(1, D), tiles)
    return out.reshape(B, S, D)


def kernel(x, mem, positions, norm1_g, w_in, ret_g, rg_conv_w, rg_conv_b, rg_wa, rg_ba, rg_wx, rg_bx, rg_lambda, w_out, norm2_g, norm_mem_g, xa_wq, xa_wk, xa_wv, xa_wo, norm3_g, ffn_w_up, ffn_conv_w, ffn_conv_b, ffn_w_down, final_g):
    return _block(x, mem, positions, norm1_g, w_in, ret_g, rg_conv_w, rg_conv_b, rg_wa, rg_ba, rg_wx, rg_bx,
                  rg_lambda, w_out, norm2_g, norm_mem_g, xa_wq, xa_wk, xa_wv, xa_wo, norm3_g, ffn_w_up,
                  ffn_conv_w, ffn_conv_b, ffn_w_down, final_g, _TILES)
```
